```python
import math
import jax
import jax.numpy as jnp
from jax import lax
import numpy as np

D_MODEL = 1024
BATCH = 2
SEQ = 8192
DEPTH = 2

HEAD_DIM = 64
A_HEADS = 16
A_WIDTH = A_HEADS * HEAD_DIM
DILATED_PAIRS = ((128, 1), (512, 4), (2048, 16))
B_HEADS = D_MODEL // (2 * HEAD_DIM)
B_QK_WIDTH = 2 * B_HEADS * HEAD_DIM
B_V_DIM = 2 * HEAD_DIM
B_WIDTH = B_HEADS * B_V_DIM
IN_WIDTHS = (A_WIDTH, A_WIDTH, A_WIDTH, A_WIDTH,
             B_QK_WIDTH, B_QK_WIDTH, B_WIDTH, B_WIDTH,
             D_MODEL, D_MODEL)
IN_TOTAL = sum(IN_WIDTHS)
ROPE_THETA = 500000.0
ROPE_DIM = HEAD_DIM // 4
Q_BLOCK = 128
RMS_EPS = 1e-6
SUBLN_EPS = 1e-5
NEG = -1e30

kernel_name = 'hybrid_dilated_diff_gated_trunk'


def rms_norm(x, w, eps):
    xf = x.astype(jnp.float32)
    return xf * lax.rsqrt(jnp.mean(xf * xf, axis=-1, keepdims=True) + eps) * w.astype(jnp.float32)


def rope_tables(seq):
    inv = 1.0 / (ROPE_THETA ** (jnp.arange(0, ROPE_DIM, 2, dtype=jnp.float32) / ROPE_DIM))
    ang = jnp.arange(seq, dtype=jnp.float32)[:, None] * inv[None, :]
    return jnp.cos(ang), jnp.sin(ang)


def partial_rope(t, cos, sin):
    half = ROPE_DIM // 2
    tr = t[..., :ROPE_DIM].astype(jnp.float32)
    t1, t2 = tr[..., :half], tr[..., half:]
    c = cos[None, :, None, :]
    s = sin[None, :, None, :]
    rot = jnp.concatenate([t1 * c - t2 * s, t1 * s + t2 * c], axis=-1).astype(t.dtype)
    return jnp.concatenate([rot, t[..., ROPE_DIM:]], axis=-1)


def split_cols(u):
    outs = []
    start = 0
    for w in IN_WIDTHS:
        outs.append(u[..., start:start + w])
        start += w
    return outs


def dilated_pattern(q, k, v, window, dilation):
    b, s, h, dh = q.shape
    n = window // dilation
    chunk = dilation * n
    sp = -(-s // chunk) * chunk
    length = sp // dilation
    nb = length // n

    def to_blocks(t):
        t = jnp.pad(t, ((0, 0), (0, sp - s), (0, 0), (0, 0)))
        t = t.reshape(b, length, dilation, h, dh).transpose(0, 2, 1, 3, 4)
        return t.reshape(b, dilation, nb, n, h, dh)

    def with_prev(t):
        prev = jnp.pad(t, ((0, 0), (0, 0), (1, 0), (0, 0), (0, 0), (0, 0)))[:, :, :-1]
        return jnp.concatenate([prev, t], axis=3)

    qb = to_blocks(q)
    kb = with_prev(to_blocks(k))
    vb = with_prev(to_blocks(v))
    scores = jnp.einsum('brnqhd,brnkhd->brnqhk', qb, kb,
                        preferred_element_type=jnp.float32) * (HEAD_DIM ** -0.5)
    qi = jnp.arange(n)[:, None]
    kc = jnp.arange(2 * n)[None, :]
    dist = qi + n - kc
    blk = jnp.arange(nb)[:, None, None]
    valid = (dist >= 0) & (dist <= n) & (blk * n + kc - n >= 0)
    scores = jnp.where(valid[None, None, :, :, None, :], scores, NEG)
    m = jnp.max(scores, axis=-1)
    p = jnp.exp(scores - m[..., None])
    den = jnp.sum(p, axis=-1)
    o = jnp.einsum('brnqhk,brnkhd->brnqhd', p, vb.astype(jnp.float32)) / den[..., None]

    def from_blocks(t):
        t = t.reshape((b, dilation, length) + t.shape[4:])
        t = jnp.swapaxes(t, 1, 2)
        return t.reshape((b, sp) + t.shape[3:])[:, :s]

    return from_blocks(o), from_blocks(m), from_blocks(den)


def dilated_mixture(q, k, v):
    outs = [dilated_pattern(q, k, v, w, d) for (w, d) in DILATED_PAIRS]
    m_all = jnp.stack([r[1] for r in outs])
    d_all = jnp.stack([r[2] for r in outs])
    o_all = jnp.stack([r[0] for r in outs])
    wts = d_all * jnp.exp(m_all - jnp.max(m_all, axis=0))
    return jnp.sum(wts[..., None] * o_all, axis=0) / jnp.sum(wts, axis=0)[..., None]


def diff_attention(q1, q2, k1, k2, v, lam):
    b, s, h, dh = q1.shape
    nq = s // Q_BLOCK
    qs = jnp.stack([q1, q2], 0).reshape(2, b, nq, Q_BLOCK, h, dh).transpose(2, 0, 1, 3, 4, 5)
    ks = jnp.stack([k1, k2], 0)
    vf = v.astype(jnp.float32)
    kpos = jnp.arange(s)

    def one_block(args):
        qblk, bi = args
        sc = jnp.einsum('mbqhd,mbkhd->mbhqk', qblk, ks,
                        preferred_element_type=jnp.float32) * (dh ** -0.5)
        qpos = bi * Q_BLOCK + jnp.arange(Q_BLOCK)
        sc = jnp.where(kpos[None, :] <= qpos[:, None], sc, NEG)
        p = jax.nn.softmax(sc, axis=-1)
        a = p[0] - lam * p[1]
        return jnp.einsum('bhqk,bkhe->bqhe', a, vf)

    out = lax.map(one_block, (qs, jnp.arange(nq)))
    return out.transpose(1, 0, 2, 3, 4).reshape(b, s, h, 2 * dh)


def lambda_init_value(layer):
    return 0.8 - 0.6 * math.exp(-0.3 * layer)


def hybrid_layer(x, layer, norm_w, w_in, lq1, lk1, lq2, lk2, subln_w,
                 w_proj_a, w_proj_b, w_out, cos, sin):
    b, s, _ = x.shape
    hdn = rms_norm(x, norm_w, RMS_EPS).astype(x.dtype)
    u = hdn @ w_in
    qa, ka, va, za, qb, kb, vb, zb, ga, gb = split_cols(u)

    qa = partial_rope(qa.reshape(b, s, A_HEADS, HEAD_DIM), cos, sin)
    ka = partial_rope(ka.reshape(b, s, A_HEADS, HEAD_DIM), cos, sin)
    va = va.reshape(b, s, A_HEADS, HEAD_DIM)
    ya = dilated_mixture(qa, ka, va).reshape(b, s, A_WIDTH)
    ya = ya * jax.nn.silu(za.astype(jnp.float32))

    qb = partial_rope(qb.reshape(b, s, 2 * B_HEADS, HEAD_DIM), cos, sin)
    kb = partial_rope(kb.reshape(b, s, 2 * B_HEADS, HEAD_DIM), cos, sin)
    vb = vb.reshape(b, s, B_HEADS, B_V_DIM)
    lam_init = lambda_init_value(layer)
    lam = (jnp.exp(jnp.sum(lq1.astype(jnp.float32) * lk1.astype(jnp.float32)))
           - jnp.exp(jnp.sum(lq2.astype(jnp.float32) * lk2.astype(jnp.float32))) + lam_init)
    ob = diff_attention(qb[:, :, 0::2], qb[:, :, 1::2], kb[:, :, 0::2], kb[:, :, 1::2], vb, lam)
    ob = rms_norm(ob, subln_w, SUBLN_EPS) * (1.0 - lam_init)
    yb = ob.reshape(b, s, B_WIDTH) * jax.nn.silu(zb.astype(jnp.float32))

    pa = ya.astype(x.dtype) @ w_proj_a
    pb = yb.astype(x.dtype) @ w_proj_b
    merged = (jax.nn.sigmoid(ga.astype(jnp.float32)) * pa.astype(jnp.float32)
              + jax.nn.sigmoid(gb.astype(jnp.float32)) * pb.astype(jnp.float32))
    return x + (merged.astype(x.dtype) @ w_out).astype(x.dtype)


def setup_inputs(seed: int = 0) -> dict:
    key = jax.random.key(seed)
    ks = jax.random.split(key, 16)
    f32 = jnp.float32
    x = jax.random.normal(ks[0], (BATCH, SEQ, D_MODEL), f32)
    norm_w = 1.0 + 0.02 * jax.random.normal(ks[1], (DEPTH, D_MODEL), f32)
    w_in = jax.random.normal(ks[2], (DEPTH, D_MODEL, IN_TOTAL), f32) * D_MODEL ** -0.5
    lambda_q1 = 0.1 * jax.random.normal(ks[3], (DEPTH, HEAD_DIM), f32)
    lambda_k1 = 0.1 * jax.random.normal(ks[4], (DEPTH, HEAD_DIM), f32)
    lambda_q2 = 0.1 * jax.random.normal(ks[5], (DEPTH, HEAD_DIM), f32)
    lambda_k2 = 0.1 * jax.random.normal(ks[6], (DEPTH, HEAD_DIM), f32)
    subln_w = 1.0 + 0.02 * jax.random.normal(ks[7], (DEPTH, B_V_DIM), f32)
    w_proj_a = jax.random.normal(ks[8], (DEPTH, A_WIDTH, D_MODEL), f32) * A_WIDTH ** -0.5
    w_proj_b = jax.random.normal(ks[9], (DEPTH, B_WIDTH, D_MODEL), f32) * B_WIDTH ** -0.5
    w_out = jax.random.normal(ks[10], (DEPTH, D_MODEL, D_MODEL), f32) * D_MODEL ** -0.5
    final_norm_w = 1.0 + 0.02 * jax.random.normal(ks[11], (D_MODEL,), f32)
    return {'x': x, 'norm_w': norm_w, 'w_in': w_in,
            'lambda_q1': lambda_q1, 'lambda_k1': lambda_k1,
            'lambda_q2': lambda_q2, 'lambda_k2': lambda_k2,
            'subln_w': subln_w, 'w_proj_a': w_proj_a, 'w_proj_b': w_proj_b,
            'w_out': w_out, 'final_norm_w': final_norm_w}


def reference(x, norm_w, w_in, lambda_q1, lambda_k1, lambda_q2, lambda_k2,
              subln_w, w_proj_a, w_proj_b, w_out, final_norm_w):
    cos, sin = rope_tables(x.shape[1])
    h = x
    for layer in range(DEPTH):
        h = hybrid_layer(h, layer, norm_w[layer], w_in[layer],
                         lambda_q1[layer], lambda_k1[layer], lambda_q2[layer], lambda_k2[layer],
                         subln_w[layer], w_proj_a[layer], w_proj_b[layer], w_out[layer],
                         cos, sin)
    return rms_norm(h, final_norm_w, RMS_EPS).astype(x.dtype)
```

```python
import functools
import math

import jax
import jax.numpy as jnp
from jax import lax
from jax.experimental import pallas as pl
from jax.experimental.pallas import tpu as pltpu

D_MODEL = 1024
HEAD_DIM = 64
A_HEADS = 16
B_HEADS = 8
DILATIONS = (1, 4, 16)
BAND = 128
ROPE_THETA = 500000.0
ROPE_DIM = HEAD_DIM // 4
RMS_EPS = 1e-6
SUBLN_EPS = 1e-5
NEG = -1e30
LANES = 128
N_COL_BLOCKS = 10
VMEM_LIMIT = 56 * 1024 * 1024

_F32 = jnp.float32
_BF16 = jnp.bfloat16


def _lambda_init(layer):
    return 0.8 - 0.6 * math.exp(-0.3 * layer)


def _rms(x, w, eps):
    return x * lax.rsqrt(jnp.mean(x * x, axis=-1, keepdims=True) + eps) * w


def _sigmoid(z):
    return 1.0 / (1.0 + jnp.exp(-z))


def _proj_kernel(x_ref, nw_ref, w_ref, cos_ref, sa_ref, sb_ref, o_ref, h_ref):
    j = pl.program_id(1)

    @pl.when(j == 0)
    def _():
        h_ref[...] = _rms(x_ref[...], nw_ref[...], RMS_EPS).astype(_BF16)

    u = jnp.dot(h_ref[...], w_ref[...], preferred_element_type=_F32)
    is_v = jnp.logical_or(j == 2, j == 5)

    @pl.when(is_v)
    def _():
        o_ref[0] = u.astype(_BF16)

    @pl.when(jnp.logical_not(is_v))
    def _():
        scale = jnp.where(jnp.logical_or(j == 0, j == 3), HEAD_DIM ** -0.5, 1.0).astype(_F32)
        c = cos_ref[...] * scale
        sa = sa_ref[...] * scale
        sb = sb_ref[...] * scale
        for blk in range(D_MODEL // LANES):
            t = u[:, blk * LANES:(blk + 1) * LANES]
            r = (t * c + pltpu.roll(t, ROPE_DIM // 2, axis=1) * sa
                 + pltpu.roll(t, LANES - ROPE_DIM // 2, axis=1) * sb)
            o_ref[0, :, blk * LANES:(blk + 1) * LANES] = r.astype(_BF16)


def _proj_call(x2, norm_w, w_in_bf, layer, tabs, seq, tm=512):
    m = x2.shape[0]
    pos_blocks = seq // tm
    cos_t, sa_t, sb_t = tabs
    col = lambda j: j + jnp.where(j >= 3, 1, 0)
    tab_spec = pl.BlockSpec((tm, LANES), lambda i, j: (i % pos_blocks, 0))
    return pl.pallas_call(
        _proj_kernel,
        out_shape=jax.ShapeDtypeStruct((6, m, D_MODEL), _BF16),
        grid=(m // tm, 6),
        in_specs=[
            pl.BlockSpec((tm, D_MODEL), lambda i, j: (i, 0)),
            pl.BlockSpec((None, 1, D_MODEL), lambda i, j: (layer, 0, 0)),
            pl.BlockSpec((None, D_MODEL, D_MODEL), lambda i, j: (layer, 0, col(j))),
            tab_spec, tab_spec, tab_spec,
        ],
        out_specs=pl.BlockSpec((1, tm, D_MODEL), lambda i, j: (j, i, 0)),
        scratch_shapes=[pltpu.VMEM((tm, D_MODEL), _BF16)],
        compiler_params=pltpu.CompilerParams(
            dimension_semantics=("parallel", "arbitrary"), vmem_limit_bytes=VMEM_LIMIT),
        name="qkv_proj",
    )(x2, norm_w, w_in_bf, cos_t, sa_t, sb_t)


def _dilated_kernel(bias_ref, q_ref, kp_ref, kc_ref, vp_ref, vc_ref, o_ref, st_ref):
    lane = lax.broadcasted_iota(jnp.int32, (BAND, LANES), 1)
    low = lane < HEAD_DIM
    bias = bias_ref[...]
    st = jnp.zeros((BAND, LANES), _F32)
    for hp in range(A_HEADS // 2):
        sl = slice(hp * LANES, (hp + 1) * LANES)
        q = q_ref[:, sl]
        zero = jnp.zeros_like(q)
        qq = jnp.concatenate([jnp.where(low, q, zero), jnp.where(low, zero, q)], axis=0)
        kk = jnp.concatenate([kp_ref[:, sl], kc_ref[:, sl]], axis=0)
        vv = jnp.concatenate([vp_ref[:, sl], vc_ref[:, sl]], axis=0)
        s = lax.dot_general(qq, kk, (((1,), (1,)), ((), ())), preferred_element_type=_F32) + bias
        m = jnp.max(s, axis=1, keepdims=True)
        p = jnp.exp(s - m)
        l = jnp.sum(p, axis=1, keepdims=True)
        pv = jnp.dot(p.astype(_BF16), vv, preferred_element_type=_F32) * (1.0 / l)
        o_ref[:, sl] = jnp.where(low, pv[:BAND], pv[BAND:])
        for e in range(2):
            h = 2 * hp + e
            st = jnp.where(lane == h, m[e * BAND:(e + 1) * BAND], st)
            st = jnp.where(lane == A_HEADS + h, l[e * BAND:(e + 1) * BAND], st)
    st_ref[...] = st


def _dilated_call(qkv, bias, dil, batch, seq):
    m = batch * seq
    nb = seq // (BAND * dil)
    view = qkv.reshape(6, m // dil, dil * D_MODEL)

    def spec(which, prev):
        def idx(b, r, n):
            nn = jnp.maximum(n - 1, 0) if prev else n
            return (which, b * nb + nn, r)
        return pl.BlockSpec((None, BAND, D_MODEL), idx)

    o, st = pl.pallas_call(
        _dilated_kernel,
        out_shape=(jax.ShapeDtypeStruct((m // dil, dil * D_MODEL), _F32),
                   jax.ShapeDtypeStruct((m // dil, dil * LANES), _F32)),
        grid=(batch, dil, nb),
        in_specs=[
            pl.BlockSpec((None, 2 * BAND, 2 * BAND), lambda b, r, n: (jnp.minimum(n, 1), 0, 0)),
            spec(0, False), spec(1, True), spec(1, False), spec(2, True), spec(2, False),
        ],
        out_specs=(pl.BlockSpec((BAND, D_MODEL), lambda b, r, n: (b * nb + n, r)),
                   pl.BlockSpec((BAND, LANES), lambda b, r, n: (b * nb + n, r))),
        compiler_params=pltpu.CompilerParams(
            dimension_semantics=("parallel", "parallel", "arbitrary"),
            vmem_limit_bytes=VMEM_LIMIT),
        name=f"dilated_d{dil}",
    )(bias, view, view, view, view, view)
    return o.reshape(m, D_MODEL), st.reshape(m, LANES)


def _band_bias():
    qi = jnp.arange(BAND)[:, None]
    kc = jnp.arange(2 * BAND)[None, :]
    ok = (kc >= qi) & (kc <= qi + BAND)
    first = ok & (kc >= BAND)
    both = jnp.stack([first, ok]).astype(_F32)
    both = jnp.concatenate([both, both], axis=1)
    return (1.0 - both) * NEG


def _diff_kernel(lam_ref, sw_ref, q_ref, k_ref, v_ref, o_ref, *, tq, tk, lam_init):
    qi = pl.program_id(2)
    lane = lax.broadcasted_iota(jnp.int32, (tq, LANES), 1)
    low = lane < HEAD_DIM
    q = q_ref[...]
    zero = jnp.zeros_like(q)
    qq = jnp.concatenate([jnp.where(low, q, zero), jnp.where(low, zero, q)], axis=0)

    def step(j, carry, masked):
        m, l, acc = carry
        start = pl.multiple_of(j * tk, tk)
        kb = k_ref[pl.ds(start, tk), :]
        vb = v_ref[pl.ds(start, tk), :]
        s = lax.dot_general(qq, kb, (((1,), (1,)), ((), ())), preferred_element_type=_F32)
        if masked:
            row = lax.broadcasted_iota(jnp.int32, (2 * tq, tk), 0)
            row = jnp.where(row >= tq, row - tq, row) + qi * tq
            colp = lax.broadcasted_iota(jnp.int32, (2 * tq, tk), 1) + j * tk
            s = jnp.where(colp <= row, s, NEG)
        m_new = jnp.maximum(m, jnp.max(s, axis=1, keepdims=True))
        alpha = jnp.exp(m - m_new)
        p = jnp.exp(s - m_new)
        l = alpha * l + jnp.sum(p, axis=1, keepdims=True)
        acc = alpha * acc + jnp.dot(p.astype(_BF16), vb, preferred_element_type=_F32)
        return m_new, l, acc

    init = (jnp.full((2 * tq, 1), NEG, _F32), jnp.zeros((2 * tq, 1), _F32),
            jnp.zeros((2 * tq, LANES), _F32))
    n_full = qi * (tq // tk)
    carry = lax.fori_loop(0, n_full, functools.partial(step, masked=False), init)
    for d in range(tq // tk):
        carry = step(n_full + d, carry, True)
    _, l, acc = carry

    lp = lam_ref[...]
    lam = (jnp.exp(jnp.sum(lp[0:1] * lp[1:2], axis=1, keepdims=True))
           - jnp.exp(jnp.sum(lp[2:3] * lp[3:4], axis=1, keepdims=True)) + lam_init)
    o = acc[:tq] * (1.0 / l[:tq]) - lam * (acc[tq:] * (1.0 / l[tq:]))
    o_ref[...] = _rms(o, sw_ref[...], SUBLN_EPS) * (1.0 - lam_init)


def _diff_call(qkv, lam_p, subln_w, layer, batch, seq, tq=512, tk=512):
    m = batch * seq
    nq = seq // tq
    kern = functools.partial(_diff_kernel, tq=tq, tk=tk, lam_init=_lambda_init(layer))
    return pl.pallas_call(
        kern,
        out_shape=jax.ShapeDtypeStruct((m, D_MODEL), _F32),
        grid=(batch, B_HEADS, nq),
        in_specs=[
            pl.BlockSpec((None, 4, HEAD_DIM), lambda b, h, i: (layer, 0, 0)),
            pl.BlockSpec((None, 1, LANES), lambda b, h, i: (layer, 0, 0)),
            pl.BlockSpec((None, tq, LANES), lambda b, h, i: (3, b * nq + i, h)),
            pl.BlockSpec((None, seq, LANES), lambda b, h, i: (4, b, h)),
            pl.BlockSpec((None, seq, LANES), lambda b, h, i: (5, b, h)),
        ],
        out_specs=pl.BlockSpec((tq, LANES), lambda b, h, i: (b * nq + i, h)),
        compiler_params=pltpu.CompilerParams(
            dimension_semantics=("parallel", "parallel", "arbitrary"),
            vmem_limit_bytes=VMEM_LIMIT),
        name="diff_attn",
    )(lam_p, subln_w, qkv, qkv, qkv)


def _expand_heads(w, e):
    hi = w.astype(_BF16)
    r1 = w - hi.astype(_F32)
    mid = r1.astype(_BF16)
    lo = (r1 - mid.astype(_F32)).astype(_BF16)
    return (jnp.dot(hi, e, preferred_element_type=_F32)
            + jnp.dot(mid, e, preferred_element_type=_F32)
            + jnp.dot(lo, e, preferred_element_type=_F32))


def _merge_kernel(x_ref, o1_ref, o2_ref, o3_ref, s1_ref, s2_ref, s3_ref, ob_ref, nw_ref,
                  wza_ref, wzb_ref, wga_ref, wgb_ref, wpa_ref, wpb_ref, wo_ref, e_ref, fw_ref,
                  out_ref, *, final):
    x = x_ref[...]
    hdn = _rms(x, nw_ref[...], RMS_EPS).astype(_BF16)

    sts = (s1_ref[...], s2_ref[...], s3_ref[...])
    lane = lax.broadcasted_iota(jnp.int32, sts[0].shape, 1)
    mx = jnp.maximum(jnp.maximum(sts[0], sts[1]), sts[2])
    wts = [pltpu.roll(s, LANES - A_HEADS, axis=1) * jnp.exp(s - mx) for s in sts]
    inv = 1.0 / (wts[0] + wts[1] + wts[2])
    e = e_ref[...]
    ya = None
    for w, o_ref in zip(wts, (o1_ref, o2_ref, o3_ref)):
        wn = jnp.where(lane < A_HEADS, w * inv, 0.0)
        term = _expand_heads(wn, e) * o_ref[...]
        ya = term if ya is None else ya + term

    za = jnp.dot(hdn, wza_ref[...], preferred_element_type=_F32)
    ya = (ya * (za * _sigmoid(za))).astype(_BF16)
    pa = jnp.dot(ya, wpa_ref[...], preferred_element_type=_F32)
    ga = jnp.dot(hdn, wga_ref[...], preferred_element_type=_F32)
    merged = _sigmoid(ga) * pa

    zb = jnp.dot(hdn, wzb_ref[...], preferred_element_type=_F32)
    yb = (ob_ref[...] * (zb * _sigmoid(zb))).astype(_BF16)
    pb = jnp.dot(yb, wpb_ref[...], preferred_element_type=_F32)
    gb = jnp.dot(hdn, wgb_ref[...], preferred_element_type=_F32)
    merged = (merged + _sigmoid(gb) * pb).astype(_BF16)

    y = x + jnp.dot(merged, wo_ref[...], preferred_element_type=_F32)
    if final:
        y = _rms(y, fw_ref[...], RMS_EPS)
    out_ref[...] = y


def _merge_call(x2, os_, sts, ob, norm_w, w_in_bf, wpa_bf, wpb_bf, wo_bf, expand, final_w,
                layer, final, tm=256):
    m = x2.shape[0]
    row = lambda width: pl.BlockSpec((tm, width), lambda i: (i, 0))
    once = pl.Buffered(1)
    w_in_col = lambda c: pl.BlockSpec((None, D_MODEL, D_MODEL), lambda i: (layer, 0, c),
                                      pipeline_mode=once)
    w_sq = pl.BlockSpec((None, D_MODEL, D_MODEL), lambda i: (layer, 0, 0), pipeline_mode=once)
    return pl.pallas_call(
        functools.partial(_merge_kernel, final=final),
        out_shape=jax.ShapeDtypeStruct((m, D_MODEL), _F32),
        grid=(m // tm,),
        in_specs=[
            row(D_MODEL), row(D_MODEL), row(D_MODEL), row(D_MODEL),
            row(LANES), row(LANES), row(LANES), row(D_MODEL),
            pl.BlockSpec((None, 1, D_MODEL), lambda i: (layer, 0, 0)),
            w_in_col(3), w_in_col(7), w_in_col(8), w_in_col(9),
            w_sq, w_sq, w_sq,
            pl.BlockSpec((LANES, D_MODEL), lambda i: (0, 0), pipeline_mode=once),
            pl.BlockSpec((1, D_MODEL), lambda i: (0, 0)),
        ],
        out_specs=row(D_MODEL),
        compiler_params=pltpu.CompilerParams(
            dimension_semantics=("parallel",), vmem_limit_bytes=VMEM_LIMIT),
        name="merge_out",
    )(x2, *os_, *sts, ob, norm_w, w_in_bf, w_in_bf, w_in_bf, w_in_bf, wpa_bf, wpb_bf, wo_bf,
      expand, final_w)


def _rope_tables(seq):
    half = ROPE_DIM // 2
    inv = 1.0 / (ROPE_THETA ** (jnp.arange(0, ROPE_DIM, 2, dtype=_F32) / ROPE_DIM))
    ang = jnp.arange(seq, dtype=_F32)[:, None] * inv[None, :]
    cos, sin = jnp.cos(ang), jnp.sin(ang)
    rest = HEAD_DIM - ROPE_DIM
    z = lambda n: jnp.zeros((seq, n), _F32)
    cos_h = jnp.concatenate([cos, cos, jnp.ones((seq, rest), _F32)], axis=1)
    sa_h = jnp.concatenate([z(half), sin, z(rest)], axis=1)
    sb_h = jnp.concatenate([-sin, z(half), z(rest)], axis=1)
    rep = LANES // HEAD_DIM
    return tuple(jnp.tile(t, (1, rep)) for t in (cos_h, sa_h, sb_h))


def kernel(x, norm_w, w_in, lambda_q1, lambda_k1, lambda_q2, lambda_k2, subln_w,
           w_proj_a, w_proj_b, w_out, final_norm_w):
    batch, seq, d = x.shape
    depth = norm_w.shape[0]
    assert d == D_MODEL and seq % (BAND * DILATIONS[-1]) == 0
    assert w_in.shape[-1] == N_COL_BLOCKS * D_MODEL

    w_in_bf = w_in.astype(_BF16)
    wpa_bf = w_proj_a.astype(_BF16)
    wpb_bf = w_proj_b.astype(_BF16)
    wo_bf = w_out.astype(_BF16)
    lam_p = jnp.stack([lambda_q1, lambda_k1, lambda_q2, lambda_k2], axis=1)
    tabs = _rope_tables(seq)
    bias = _band_bias()
    head_of_col = jnp.arange(D_MODEL)[None, :] // HEAD_DIM
    expand = (jnp.arange(LANES)[:, None] == head_of_col).astype(_BF16)
    final_w = final_norm_w.reshape(1, D_MODEL)
    norm_w = norm_w.reshape(depth, 1, D_MODEL)
    subln_w = subln_w.reshape(depth, 1, LANES)

    h = x.reshape(batch * seq, D_MODEL)
    for layer in range(depth):
        qkv = _proj_call(h, norm_w, w_in_bf, layer, tabs, seq)
        os_, sts = zip(*[_dilated_call(qkv, bias, dil, batch, seq) for dil in DILATIONS])
        ob = _diff_call(qkv, lam_p, subln_w, layer, batch, seq)
        h = _merge_call(h, os_, sts, ob, norm_w, w_in_bf, wpa_bf, wpb_bf, wo_bf, expand,
                        final_w, layer, final=(layer == depth - 1))
    return h.reshape(batch, seq, D_MODEL)
```

```python
import functools
import math

import jax
import jax.numpy as jnp
from jax import lax
from jax.experimental import pallas as pl
from jax.experimental.pallas import tpu as pltpu

D_MODEL = 1024
HEAD_DIM = 64
A_HEADS = 16
B_HEADS = 8
DILATIONS = (1, 4, 16)
BAND = 128
ROPE_THETA = 500000.0
ROPE_DIM = HEAD_DIM // 4
RMS_EPS = 1e-6
SUBLN_EPS = 1e-5
NEG = -1e30
LANES = 128
N_COL_BLOCKS = 10
VMEM_LIMIT = 56 * 1024 * 1024
DIFF_TQ = 1024
DIFF_TK = 512
ONES_ROWS = 16

_F32 = jnp.float32
_BF16 = jnp.bfloat16


def _lambda_init(layer):
    return 0.8 - 0.6 * math.exp(-0.3 * layer)


def _rms(x, w, eps):
    return x * lax.rsqrt(jnp.mean(x * x, axis=-1, keepdims=True) + eps) * w


def _sigmoid(z):
    return 1.0 / (1.0 + jnp.exp(-z))


def _proj_kernel(x_ref, nw_ref, wqa_ref, wka_ref, wva_ref, wqb_ref, wkb_ref, wvb_ref,
                 cos_ref, sa_ref, sb_ref, a1_ref, a4_ref, a16_ref, qk_ref, vt_ref,
                 scr0_ref, scr1_ref):
    tm = x_ref.shape[0]
    hdn = _rms(x_ref[...], nw_ref[...], RMS_EPS).astype(_BF16)
    cos, sa, sb = cos_ref[...], sa_ref[...], sb_ref[...]
    q_scale = HEAD_DIM ** -0.5
    tabs = {True: (cos * q_scale, sa * q_scale, sb * q_scale), False: (cos, sa, sb)}

    def lane_blocks(u):
        return [u[:, blk * LANES:(blk + 1) * LANES] for blk in range(D_MODEL // LANES)]

    def rope(u, is_q):
        c, s_a, s_b = tabs[is_q]
        return [t * c + pltpu.roll(t, ROPE_DIM // 2, axis=1) * s_a
                + pltpu.roll(t, LANES - ROPE_DIM // 2, axis=1) * s_b for t in lane_blocks(u)]

    def residue_layouts(scr, blocks, which):
        for blk, t in enumerate(blocks):
            sl = slice(blk * LANES, (blk + 1) * LANES)
            a1_ref[which, :, sl] = t.astype(_BF16)
            scr[blk] = t
            for d, ref in ((DILATIONS[1], a4_ref), (DILATIONS[2], a16_ref)):
                for r in range(d):
                    ref[which, r, :, sl] = scr[blk, pl.ds(r, tm // d, stride=d), :].astype(_BF16)

    def natural(ref, blocks, which):
        for blk, t in enumerate(blocks):
            ref[which, :, blk * LANES:(blk + 1) * LANES] = t.astype(_BF16)

    dot = lambda w_ref: jnp.dot(hdn, w_ref[...], preferred_element_type=_F32)
    residue_layouts(scr0_ref, rope(dot(wqa_ref), True), 0)
    residue_layouts(scr1_ref, rope(dot(wka_ref), False), 1)
    residue_layouts(scr0_ref, lane_blocks(dot(wva_ref)), 2)
    natural(qk_ref, rope(dot(wqb_ref), True), 0)
    natural(qk_ref, rope(dot(wkb_ref), False), 1)
    vt_ref[0] = dot(wvb_ref).T.astype(_BF16)


def _proj_call(x2, norm_w, w_in_bf, layer, tabs, batch, seq, tm=DIFF_TK):
    m = x2.shape[0]
    tiles = seq // tm
    cos_t, sa_t, sb_t = tabs
    once = pl.Buffered(1)
    w_col = lambda c: pl.BlockSpec((None, D_MODEL, D_MODEL), lambda i: (layer, 0, c),
                                   pipeline_mode=once)
    tab_spec = pl.BlockSpec((tm, LANES), lambda i: (i % tiles, 0))
    a_shape = lambda d: jax.ShapeDtypeStruct((3, batch, d, seq // d, D_MODEL), _BF16)
    a_spec = lambda d: pl.BlockSpec((3, None, d, tm // d, D_MODEL),
                                    lambda i: (0, i // tiles, 0, i % tiles, 0))
    d1, d4, d16 = DILATIONS
    return pl.pallas_call(
        _proj_kernel,
        out_shape=(a_shape(d1), a_shape(d4), a_shape(d16),
                   jax.ShapeDtypeStruct((2, m, D_MODEL), _BF16),
                   jax.ShapeDtypeStruct((m // tm, D_MODEL, tm), _BF16)),
        grid=(m // tm,),
        in_specs=[
            pl.BlockSpec((tm, D_MODEL), lambda i: (i, 0)),
            pl.BlockSpec((None, 1, D_MODEL), lambda i: (layer, 0, 0)),
            w_col(0), w_col(1), w_col(2), w_col(4), w_col(5), w_col(6),
            tab_spec, tab_spec, tab_spec,
        ],
        out_specs=(pl.BlockSpec((3, None, None, tm, D_MODEL),
                                lambda i: (0, i // tiles, 0, i % tiles, 0)),
                   a_spec(d4), a_spec(d16),
                   pl.BlockSpec((2, tm, D_MODEL), lambda i: (0, i, 0)),
                   pl.BlockSpec((1, D_MODEL, tm), lambda i: (i, 0, 0))),
        scratch_shapes=[pltpu.VMEM((D_MODEL // LANES, tm, LANES), _F32)] * 2,
        compiler_params=pltpu.CompilerParams(
            dimension_semantics=("parallel",), vmem_limit_bytes=VMEM_LIMIT),
        name="qkv_proj",
    )(x2, norm_w, w_in_bf, w_in_bf, w_in_bf, w_in_bf, w_in_bf, w_in_bf, cos_t, sa_t, sb_t)


def _dilated_kernel(bias_ref, q_ref, kp_ref, kc_ref, vp_ref, vc_ref, o_ref, st_ref):
    lane = lax.broadcasted_iota(jnp.int32, (BAND, LANES), 1)
    low = lane < HEAD_DIM
    bias = bias_ref[...]
    st = jnp.zeros((BAND, LANES), _F32)
    for hp in range(A_HEADS // 2):
        sl = slice(hp * LANES, (hp + 1) * LANES)
        q = q_ref[:, sl]
        zero = jnp.zeros_like(q)
        qq = jnp.concatenate([jnp.where(low, q, zero), jnp.where(low, zero, q)], axis=0)
        kk = jnp.concatenate([kp_ref[:, sl], kc_ref[:, sl]], axis=0)
        vv = jnp.concatenate([vp_ref[:, sl], vc_ref[:, sl]], axis=0)
        s = lax.dot_general(qq, kk, (((1,), (1,)), ((), ())), preferred_element_type=_F32) + bias
        m = jnp.max(s, axis=1, keepdims=True)
        p = jnp.exp(s - m)
        l = jnp.sum(p, axis=1, keepdims=True)
        pv = jnp.dot(p.astype(_BF16), vv, preferred_element_type=_F32) * (1.0 / l)
        o_ref[:, sl] = jnp.where(low, pv[:BAND], pv[BAND:])
        for e in range(2):
            h = 2 * hp + e
            st = jnp.where(lane == h, m[e * BAND:(e + 1) * BAND], st)
            st = jnp.where(lane == A_HEADS + h, l[e * BAND:(e + 1) * BAND], st)
    st_ref[...] = st


def _dilated_call(qkv, bias, dil, batch, seq):
    per_class = seq // dil
    nb = per_class // BAND

    def spec(which, prev):
        def idx(b, r, n):
            return (which, b, r, jnp.maximum(n - 1, 0) if prev else n, 0)
        return pl.BlockSpec((None, None, None, BAND, D_MODEL), idx)

    return pl.pallas_call(
        _dilated_kernel,
        out_shape=(jax.ShapeDtypeStruct((batch, dil, per_class, D_MODEL), _F32),
                   jax.ShapeDtypeStruct((batch, dil, per_class, LANES), _F32)),
        grid=(batch, dil, nb),
        in_specs=[
            pl.BlockSpec((None, 2 * BAND, 2 * BAND), lambda b, r, n: (jnp.minimum(n, 1), 0, 0)),
            spec(0, False), spec(1, True), spec(1, False), spec(2, True), spec(2, False),
        ],
        out_specs=(pl.BlockSpec((None, None, BAND, D_MODEL), lambda b, r, n: (b, r, n, 0)),
                   pl.BlockSpec((None, None, BAND, LANES), lambda b, r, n: (b, r, n, 0))),
        compiler_params=pltpu.CompilerParams(
            dimension_semantics=("parallel", "parallel", "arbitrary"),
            vmem_limit_bytes=VMEM_LIMIT),
        name=f"dilated_d{dil}",
    )(bias, qkv, qkv, qkv, qkv, qkv)


def _band_bias():
    qi = jnp.arange(BAND)[:, None]
    kc = jnp.arange(2 * BAND)[None, :]
    ok = (kc >= qi) & (kc <= qi + BAND)
    first = ok & (kc >= BAND)
    both = jnp.stack([first, ok]).astype(_F32)
    both = jnp.concatenate([both, both], axis=1)
    return (1.0 - both) * NEG


def _diff_kernel(lam_ref, sw_ref, q_ref, k_ref, vt_ref, o_ref, sa_ref, sb_ref, m_ref, acc_ref,
                 *, tq, tk, lam_init):
    qi = pl.program_id(2)
    lane = lax.broadcasted_iota(jnp.int32, (tq, LANES), 1)
    low = lane < HEAD_DIM
    q = q_ref[...]
    zero = jnp.zeros_like(q)
    qq = jnp.concatenate([jnp.where(low, q, zero), jnp.where(low, zero, q)], axis=0)
    ones = jnp.ones((ONES_ROWS, tk), _BF16)
    sub_blocks = tq // tk

    def scores(j, buf):
        start = j * tk if isinstance(j, int) else pl.multiple_of(j * tk, tk)
        kb = k_ref[pl.ds(start, tk), :]
        buf[...] = lax.dot_general(kb, qq, (((1,), (1,)), ((), ())),
                                   preferred_element_type=_F32)

    def update(j, buf, diag):
        s = buf[...]
        if diag is not None:
            key = lax.broadcasted_iota(jnp.int32, (tk, 2 * tq), 0) + diag * tk
            qry = lax.broadcasted_iota(jnp.int32, (tk, 2 * tq), 1)
            qry = jnp.where(qry >= tq, qry - tq, qry)
            s = jnp.where(key <= qry, s, NEG)
        m = m_ref[...]
        m_new = jnp.maximum(m, jnp.max(s, axis=0, keepdims=True))
        alpha = jnp.exp(m - m_new)
        p = jnp.exp(s - m_new).astype(_BF16)
        va = jnp.concatenate([vt_ref[j], ones], axis=0)
        acc_ref[...] = alpha * acc_ref[...] + jnp.dot(va, p, preferred_element_type=_F32)
        m_ref[...] = m_new

    m_ref[...] = jnp.full(m_ref.shape, NEG, _F32)
    acc_ref[...] = jnp.zeros(acc_ref.shape, _F32)
    scores(0, sa_ref)

    def body(t, carry):
        j = sub_blocks * t
        for d in range(sub_blocks):
            nxt, cur = (sb_ref, sa_ref) if d % 2 == 0 else (sa_ref, sb_ref)
            scores(j + d + 1, nxt)
            update(j + d, cur, None)
        return carry

    lax.fori_loop(0, qi, body, 0)
    for d in range(sub_blocks):
        nxt, cur = (sb_ref, sa_ref) if d % 2 == 0 else (sa_ref, sb_ref)
        if d + 1 < sub_blocks:
            scores(sub_blocks * qi + d + 1, nxt)
        update(sub_blocks * qi + d, cur, d)

    acc = acc_ref[...]
    lp = lam_ref[...]
    lam = (jnp.exp(jnp.sum(lp[0:1] * lp[1:2], axis=1, keepdims=True))
           - jnp.exp(jnp.sum(lp[2:3] * lp[3:4], axis=1, keepdims=True)) + lam_init)
    num, den = acc[:LANES], acc[LANES:LANES + 1]
    ot = num[:, :tq] / den[:, :tq] - lam * (num[:, tq:] / den[:, tq:])
    o = ot.T
    o_ref[...] = _rms(o, sw_ref[...], SUBLN_EPS) * (1.0 - lam_init)


def _diff_call(qk, vt, lam_p, subln_w, layer, batch, seq, tq=DIFF_TQ, tk=DIFF_TK):
    assert tq % tk == 0 and (tq // tk) % 2 == 0
    m = batch * seq
    nq = seq // tq
    nk = seq // tk
    kern = functools.partial(_diff_kernel, tq=tq, tk=tk, lam_init=_lambda_init(layer))
    return pl.pallas_call(
        kern,
        out_shape=jax.ShapeDtypeStruct((m, D_MODEL), _F32),
        grid=(batch, B_HEADS, nq),
        in_specs=[
            pl.BlockSpec((None, 4, HEAD_DIM), lambda b, h, i: (layer, 0, 0)),
            pl.BlockSpec((None, 1, LANES), lambda b, h, i: (layer, 0, 0)),
            pl.BlockSpec((None, tq, LANES), lambda b, h, i: (0, b * nq + i, h)),
            pl.BlockSpec((None, seq, LANES), lambda b, h, i: (1, b, h)),
            pl.BlockSpec((nk, LANES, tk), lambda b, h, i: (b, h, 0)),
        ],
        out_specs=pl.BlockSpec((tq, LANES), lambda b, h, i: (b * nq + i, h)),
        scratch_shapes=[pltpu.VMEM((tk, 2 * tq), _F32), pltpu.VMEM((tk, 2 * tq), _F32),
                        pltpu.VMEM((1, 2 * tq), _F32),
                        pltpu.VMEM((LANES + ONES_ROWS, 2 * tq), _F32)],
        compiler_params=pltpu.CompilerParams(
            dimension_semantics=("parallel", "parallel", "arbitrary"),
            vmem_limit_bytes=VMEM_LIMIT),
        name="diff_attn",
    )(lam_p, subln_w, qk, qk, vt)


def _expand_heads(w, e):
    hi = w.astype(_BF16)
    r1 = w - hi.astype(_F32)
    mid = r1.astype(_BF16)
    lo = (r1 - mid.astype(_F32)).astype(_BF16)
    return (jnp.dot(hi, e, preferred_element_type=_F32)
            + jnp.dot(mid, e, preferred_element_type=_F32)
            + jnp.dot(lo, e, preferred_element_type=_F32))


def _merge_kernel(x_ref, o1_ref, o2_ref, o3_ref, s1_ref, s2_ref, s3_ref, ob_ref, nw_ref,
                  wza_ref, wzb_ref, wga_ref, wgb_ref, wpa_ref, wpb_ref, wo_ref, e_ref, fw_ref,
                  out_ref, on2_ref, on3_ref, sn2_ref, sn3_ref, *, final):
    x = x_ref[...]
    tm = x.shape[0]
    hdn = _rms(x, nw_ref[...], RMS_EPS).astype(_BF16)

    n_blk = D_MODEL // LANES
    for d, src, dst in ((DILATIONS[1], o2_ref, on2_ref), (DILATIONS[2], o3_ref, on3_ref)):
        for r in range(d):
            for blk in range(n_blk):
                dst[blk, pl.ds(r, tm // d, stride=d), :] = src[r, :, blk * LANES:(blk + 1) * LANES]
    for d, src, dst in ((DILATIONS[1], s2_ref, sn2_ref), (DILATIONS[2], s3_ref, sn3_ref)):
        for r in range(d):
            dst[pl.ds(r, tm // d, stride=d), :] = src[r]
    o1 = o1_ref[...]
    outs = ([o1[:, blk * LANES:(blk + 1) * LANES] for blk in range(n_blk)],
            [on2_ref[blk] for blk in range(n_blk)], [on3_ref[blk] for blk in range(n_blk)])

    sts = (s1_ref[...], sn2_ref[...], sn3_ref[...])
    lane = lax.broadcasted_iota(jnp.int32, sts[0].shape, 1)
    mx = jnp.maximum(jnp.maximum(sts[0], sts[1]), sts[2])
    wts = [pltpu.roll(s, LANES - A_HEADS, axis=1) * jnp.exp(s - mx) for s in sts]
    inv = 1.0 / (wts[0] + wts[1] + wts[2])
    e = e_ref[...]
    ya_blocks = None
    for w, o_blocks in zip(wts, outs):
        wn = _expand_heads(jnp.where(lane < A_HEADS, w * inv, 0.0), e)
        terms = [wn[:, blk * LANES:(blk + 1) * LANES] * o_blocks[blk] for blk in range(n_blk)]
        ya_blocks = terms if ya_blocks is None else [a + t for a, t in zip(ya_blocks, terms)]
    ya = jnp.concatenate(ya_blocks, axis=1)

    za = jnp.dot(hdn, wza_ref[...], preferred_element_type=_F32)
    ya = (ya * (za * _sigmoid(za))).astype(_BF16)
    pa = jnp.dot(ya, wpa_ref[...], preferred_element_type=_F32)
    ga = jnp.dot(hdn, wga_ref[...], preferred_element_type=_F32)
    merged = _sigmoid(ga) * pa

    zb = jnp.dot(hdn, wzb_ref[...], preferred_element_type=_F32)
    yb = (ob_ref[...] * (zb * _sigmoid(zb))).astype(_BF16)
    pb = jnp.dot(yb, wpb_ref[...], preferred_element_type=_F32)
    gb = jnp.dot(hdn, wgb_ref[...], preferred_element_type=_F32)
    merged = (merged + _sigmoid(gb) * pb).astype(_BF16)

    y = x + jnp.dot(merged, wo_ref[...], preferred_element_type=_F32)
    if final:
        y = _rms(y, fw_ref[...], RMS_EPS)
    out_ref[...] = y


def _merge_call(x2, os_, sts, ob, norm_w, w_in_bf, wpa_bf, wpb_bf, wo_bf, expand, final_w,
                layer, final, seq, tm=256):
    m = x2.shape[0]
    tiles = seq // tm
    row = lambda width: pl.BlockSpec((tm, width), lambda i: (i, 0))
    pat = lambda d, width: pl.BlockSpec((None, d, tm // d, width),
                                        lambda i: (i // tiles, 0, i % tiles, 0))
    d1, d4, d16 = DILATIONS
    o1 = os_[0].reshape(m, D_MODEL)
    s1 = sts[0].reshape(m, LANES)
    once = pl.Buffered(1)
    w_in_col = lambda c: pl.BlockSpec((None, D_MODEL, D_MODEL), lambda i: (layer, 0, c),
                                      pipeline_mode=once)
    w_sq = pl.BlockSpec((None, D_MODEL, D_MODEL), lambda i: (layer, 0, 0), pipeline_mode=once)
    return pl.pallas_call(
        functools.partial(_merge_kernel, final=final),
        out_shape=jax.ShapeDtypeStruct((m, D_MODEL), _F32),
        grid=(m // tm,),
        in_specs=[
            row(D_MODEL), row(D_MODEL), pat(d4, D_MODEL), pat(d16, D_MODEL),
            row(LANES), pat(d4, LANES), pat(d16, LANES), row(D_MODEL),
            pl.BlockSpec((None, 1, D_MODEL), lambda i: (layer, 0, 0)),
            w_in_col(3), w_in_col(7), w_in_col(8), w_in_col(9),
            w_sq, w_sq, w_sq,
            pl.BlockSpec((LANES, D_MODEL), lambda i: (0, 0), pipeline_mode=once),
            pl.BlockSpec((1, D_MODEL), lambda i: (0, 0)),
        ],
        out_specs=row(D_MODEL),
        scratch_shapes=[pltpu.VMEM((D_MODEL // LANES, tm, LANES), _F32)] * 2
                       + [pltpu.VMEM((tm, LANES), _F32)] * 2,
        compiler_params=pltpu.CompilerParams(
            dimension_semantics=("parallel",), vmem_limit_bytes=VMEM_LIMIT),
        name="merge_out",
    )(x2, o1, os_[1], os_[2], s1, sts[1], sts[2], ob, norm_w, w_in_bf, w_in_bf, w_in_bf, w_in_bf,
      wpa_bf, wpb_bf, wo_bf, expand, final_w)


def _rope_tables(seq):
    half = ROPE_DIM // 2
    inv = 1.0 / (ROPE_THETA ** (jnp.arange(0, ROPE_DIM, 2, dtype=_F32) / ROPE_DIM))
    ang = jnp.arange(seq, dtype=_F32)[:, None] * inv[None, :]
    cos, sin = jnp.cos(ang), jnp.sin(ang)
    rest = HEAD_DIM - ROPE_DIM
    z = lambda n: jnp.zeros((seq, n), _F32)
    cos_h = jnp.concatenate([cos, cos, jnp.ones((seq, rest), _F32)], axis=1)
    sa_h = jnp.concatenate([z(half), sin, z(rest)], axis=1)
    sb_h = jnp.concatenate([-sin, z(half), z(rest)], axis=1)
    rep = LANES // HEAD_DIM
    return tuple(jnp.tile(t, (1, rep)) for t in (cos_h, sa_h, sb_h))


def kernel(x, norm_w, w_in, lambda_q1, lambda_k1, lambda_q2, lambda_k2, subln_w,
           w_proj_a, w_proj_b, w_out, final_norm_w):
    batch, seq, d = x.shape
    depth = norm_w.shape[0]
    assert d == D_MODEL and seq % (BAND * DILATIONS[-1]) == 0
    assert w_in.shape[-1] == N_COL_BLOCKS * D_MODEL

    w_in_bf = w_in.astype(_BF16)
    wpa_bf = w_proj_a.astype(_BF16)
    wpb_bf = w_proj_b.astype(_BF16)
    wo_bf = w_out.astype(_BF16)
    lam_p = jnp.stack([lambda_q1, lambda_k1, lambda_q2, lambda_k2], axis=1)
    tabs = _rope_tables(seq)
    bias = _band_bias()
    head_of_col = jnp.arange(D_MODEL)[None, :] // HEAD_DIM
    expand = (jnp.arange(LANES)[:, None] == head_of_col).astype(_BF16)
    final_w = final_norm_w.reshape(1, D_MODEL)
    norm_w = norm_w.reshape(depth, 1, D_MODEL)
    subln_w = subln_w.reshape(depth, 1, LANES)

    h = x.reshape(batch * seq, D_MODEL)
    for layer in range(depth):
        *a_layouts, qk, vt = _proj_call(h, norm_w, w_in_bf, layer, tabs, batch, seq)
        os_, sts = zip(*[_dilated_call(a, bias, dil, batch, seq)
                         for a, dil in zip(a_layouts, DILATIONS)])
        ob = _diff_call(qk, vt, lam_p, subln_w, layer, batch, seq)
        h = _merge_call(h, os_, sts, ob, norm_w, w_in_bf, wpa_bf, wpb_bf, wo_bf, expand,
                        final_w, layer, layer == depth - 1, seq)
    return h.reshape(batch, seq, D_MODEL)
```

```python
import functools
import math

import jax
import jax.numpy as jnp
from jax import lax
from jax.experimental import pallas as pl
from jax.experimental.pallas import tpu as pltpu

D_MODEL = 1024
HEAD_DIM = 64
A_HEADS = 16
B_HEADS = 8
DILATIONS = (1, 4, 16)
BAND = 128
ROPE_THETA = 500000.0
ROPE_DIM = HEAD_DIM // 4
RMS_EPS = 1e-6
SUBLN_EPS = 1e-5
NEG = -1e30
LANES = 128
N_COL_BLOCKS = 10
VMEM_LIMIT = 56 * 1024 * 1024
DIFF_TQ = 1024
DIFF_TK = 512
ONES_ROWS = 16

_F32 = jnp.float32
_BF16 = jnp.bfloat16


def _lambda_init(layer):
    return 0.8 - 0.6 * math.exp(-0.3 * layer)


def _rms(x, w, eps):
    return x * lax.rsqrt(jnp.mean(x * x, axis=-1, keepdims=True) + eps) * w


def _sigmoid(z):
    return 1.0 / (1.0 + jnp.exp(-z))


def _proj_kernel(x_ref, nw_ref, wqa_ref, wka_ref, wva_ref, wqb_ref, wkb_ref, wvb_ref,
                 cos_ref, sa_ref, sb_ref, a1_ref, a4_ref, a16_ref, qk_ref, vt_ref,
                 scr0_ref, scr1_ref):
    tm = x_ref.shape[0]
    hdn = _rms(x_ref[...], nw_ref[...], RMS_EPS).astype(_BF16)
    cos, sa, sb = cos_ref[...], sa_ref[...], sb_ref[...]
    q_scale = HEAD_DIM ** -0.5
    tabs = {True: (cos * q_scale, sa * q_scale, sb * q_scale), False: (cos, sa, sb)}

    def lane_blocks(u):
        return [u[:, blk * LANES:(blk + 1) * LANES] for blk in range(D_MODEL // LANES)]

    def rope(u, is_q):
        c, s_a, s_b = tabs[is_q]
        return [t * c + pltpu.roll(t, ROPE_DIM // 2, axis=1) * s_a
                + pltpu.roll(t, LANES - ROPE_DIM // 2, axis=1) * s_b for t in lane_blocks(u)]

    d4, d16 = DILATIONS[1], DILATIONS[2]
    step = d16 // d4

    def residue_layouts(blocks, which):
        for blk, t in enumerate(blocks):
            sl = slice(blk * LANES, (blk + 1) * LANES)
            a1_ref[which, :, sl] = t.astype(_BF16)
            scr0_ref[blk] = t
            for c in range(d4):
                x = scr0_ref[blk, pl.ds(c, tm // d4, stride=d4), :]
                a4_ref[which, c, :, sl] = x.astype(_BF16)
                scr1_ref[blk, c * (tm // d4):(c + 1) * (tm // d4), :] = x
            for r in range(d16):
                c, b = r % d4, r // d4
                x = scr1_ref[blk, pl.ds(c * (tm // d4) + b, tm // d16, stride=step), :]
                a16_ref[which, r, :, sl] = x.astype(_BF16)

    def natural(ref, blocks, which):
        for blk, t in enumerate(blocks):
            ref[which, :, blk * LANES:(blk + 1) * LANES] = t.astype(_BF16)

    dot = lambda w_ref: jnp.dot(hdn, w_ref[...], preferred_element_type=_F32)
    residue_layouts(rope(dot(wqa_ref), True), 0)
    residue_layouts(rope(dot(wka_ref), False), 1)
    residue_layouts(lane_blocks(dot(wva_ref)), 2)
    natural(qk_ref, rope(dot(wqb_ref), True), 0)
    natural(qk_ref, rope(dot(wkb_ref), False), 1)
    vt_ref[0] = dot(wvb_ref).T.astype(_BF16)


def _proj_call(x2, norm_w, w_in_bf, layer, tabs, batch, seq, tm=DIFF_TK):
    m = x2.shape[0]
    tiles = seq // tm
    cos_t, sa_t, sb_t = tabs
    once = pl.Buffered(1)
    w_col = lambda c: pl.BlockSpec((None, D_MODEL, D_MODEL), lambda i: (layer, 0, c),
                                   pipeline_mode=once)
    tab_spec = pl.BlockSpec((tm, LANES), lambda i: (i % tiles, 0))
    a_shape = lambda d: jax.ShapeDtypeStruct((3, batch, d, seq // d, D_MODEL), _BF16)
    a_spec = lambda d: pl.BlockSpec((3, None, d, tm // d, D_MODEL),
                                    lambda i: (0, i // tiles, 0, i % tiles, 0))
    d1, d4, d16 = DILATIONS
    return pl.pallas_call(
        _proj_kernel,
        out_shape=(a_shape(d1), a_shape(d4), a_shape(d16),
                   jax.ShapeDtypeStruct((2, m, D_MODEL), _BF16),
                   jax.ShapeDtypeStruct((m // tm, D_MODEL, tm), _BF16)),
        grid=(m // tm,),
        in_specs=[
            pl.BlockSpec((tm, D_MODEL), lambda i: (i, 0)),
            pl.BlockSpec((None, 1, D_MODEL), lambda i: (layer, 0, 0)),
            w_col(0), w_col(1), w_col(2), w_col(4), w_col(5), w_col(6),
            tab_spec, tab_spec, tab_spec,
        ],
        out_specs=(pl.BlockSpec((3, None, None, tm, D_MODEL),
                                lambda i: (0, i // tiles, 0, i % tiles, 0)),
                   a_spec(d4), a_spec(d16),
                   pl.BlockSpec((2, tm, D_MODEL), lambda i: (0, i, 0)),
                   pl.BlockSpec((1, D_MODEL, tm), lambda i: (i, 0, 0))),
        scratch_shapes=[pltpu.VMEM((D_MODEL // LANES, tm, LANES), _F32)] * 2,
        compiler_params=pltpu.CompilerParams(
            dimension_semantics=("parallel",), vmem_limit_bytes=VMEM_LIMIT),
        name="qkv_proj",
    )(x2, norm_w, w_in_bf, w_in_bf, w_in_bf, w_in_bf, w_in_bf, w_in_bf, cos_t, sa_t, sb_t)


def _dilated_kernel(bias_ref, q_ref, kp_ref, kc_ref, vp_ref, vc_ref, o_ref, st_ref):
    lane = lax.broadcasted_iota(jnp.int32, (BAND, LANES), 1)
    low = lane < HEAD_DIM
    bias = bias_ref[...]
    st = jnp.zeros((BAND, LANES), _F32)
    ones = jnp.ones((2 * BAND, LANES), _BF16)
    for hp in range(A_HEADS // 2):
        sl = slice(hp * LANES, (hp + 1) * LANES)
        q = q_ref[:, sl]
        zero = jnp.zeros_like(q)
        qq = jnp.concatenate([jnp.where(low, q, zero), jnp.where(low, zero, q)], axis=0)
        kk = jnp.concatenate([kp_ref[:, sl], kc_ref[:, sl]], axis=0)
        vv = jnp.concatenate([vp_ref[:, sl], vc_ref[:, sl]], axis=0)
        va = jnp.concatenate([vv, ones], axis=1)
        s = lax.dot_general(qq, kk, (((1,), (1,)), ((), ())), preferred_element_type=_F32) + bias
        m = jnp.max(s, axis=1, keepdims=True)
        p = jnp.exp(s - m).astype(_BF16)
        pv = jnp.dot(p, va, preferred_element_type=_F32)
        num, den = pv[:, :LANES], pv[:, LANES:]
        o = num / den
        o_ref[:, sl] = jnp.where(low, o[:BAND], o[BAND:])
        for e in range(2):
            h = 2 * hp + e
            st = jnp.where(lane == h, m[e * BAND:(e + 1) * BAND], st)
            st = jnp.where(lane == A_HEADS + h, den[e * BAND:(e + 1) * BAND], st)
    st_ref[...] = st


def _dilated_call(qkv, bias, dil, batch, seq):
    per_class = seq // dil
    nb = per_class // BAND

    def spec(which, prev):
        def idx(b, r, n):
            return (which, b, r, jnp.maximum(n - 1, 0) if prev else n, 0)
        return pl.BlockSpec((None, None, None, BAND, D_MODEL), idx)

    return pl.pallas_call(
        _dilated_kernel,
        out_shape=(jax.ShapeDtypeStruct((batch, dil, per_class, D_MODEL), _F32),
                   jax.ShapeDtypeStruct((batch, dil, per_class, LANES), _F32)),
        grid=(batch, dil, nb),
        in_specs=[
            pl.BlockSpec((None, 2 * BAND, 2 * BAND), lambda b, r, n: (jnp.minimum(n, 1), 0, 0)),
            spec(0, False), spec(1, True), spec(1, False), spec(2, True), spec(2, False),
        ],
        out_specs=(pl.BlockSpec((None, None, BAND, D_MODEL), lambda b, r, n: (b, r, n, 0)),
                   pl.BlockSpec((None, None, BAND, LANES), lambda b, r, n: (b, r, n, 0))),
        compiler_params=pltpu.CompilerParams(
            dimension_semantics=("parallel", "parallel", "arbitrary"),
            vmem_limit_bytes=VMEM_LIMIT),
        name=f"dilated_d{dil}",
    )(bias, qkv, qkv, qkv, qkv, qkv)


def _band_bias():
    qi = jnp.arange(BAND)[:, None]
    kc = jnp.arange(2 * BAND)[None, :]
    ok = (kc >= qi) & (kc <= qi + BAND)
    first = ok & (kc >= BAND)
    both = jnp.stack([first, ok]).astype(_F32)
    both = jnp.concatenate([both, both], axis=1)
    return (1.0 - both) * NEG


def _diff_kernel(lam_ref, sw_ref, q_ref, k_ref, vt_ref, o_ref, sa_ref, sb_ref, m_ref, acc_ref,
                 *, tq, tk, lam_init):
    qi = pl.program_id(2)
    lane = lax.broadcasted_iota(jnp.int32, (tq, LANES), 1)
    low = lane < HEAD_DIM
    q = q_ref[...]
    zero = jnp.zeros_like(q)
    qq = jnp.concatenate([jnp.where(low, q, zero), jnp.where(low, zero, q)], axis=0)
    ones = jnp.ones((ONES_ROWS, tk), _BF16)
    sub_blocks = tq // tk

    def live(x, lo, axis):
        if lo == 0:
            return x
        take = lambda a, b: lax.slice_in_dim(x, a, b, axis=axis)
        return jnp.concatenate([take(lo, tq), take(tq + lo, 2 * tq)], axis=axis)

    def put(ref, val, lo):
        if lo == 0:
            ref[...] = val
        else:
            ref[:, lo:tq] = val[:, :tq - lo]
            ref[:, tq + lo:] = val[:, tq - lo:]

    def scores(j, buf, lo=0):
        start = j * tk if isinstance(j, int) else pl.multiple_of(j * tk, tk)
        kb = k_ref[pl.ds(start, tk), :]
        buf[:, :2 * (tq - lo)] = lax.dot_general(kb, live(qq, lo, 0), (((1,), (1,)), ((), ())),
                                                 preferred_element_type=_F32)
    def update(j, buf, diag):
        lo = 0 if diag is None else diag * tk
        n = tq - lo
        s = buf[:, :2 * n]
        if diag is not None:
            key = lax.broadcasted_iota(jnp.int32, (tk, 2 * n), 0)
            qry = lax.broadcasted_iota(jnp.int32, (tk, 2 * n), 1)
            qry = jnp.where(qry >= n, qry - n, qry)
            s = jnp.where(key <= qry, s, NEG)
        m = live(m_ref[...], lo, 1)
        m_new = jnp.maximum(m, jnp.max(s, axis=0, keepdims=True))
        alpha = jnp.exp(m - m_new)
        p = jnp.exp(s - m_new).astype(_BF16)
        va = jnp.concatenate([vt_ref[j], ones], axis=0)
        acc = live(acc_ref[...], lo, 1)
        put(acc_ref, alpha * acc + jnp.dot(va, p, preferred_element_type=_F32), lo)
        put(m_ref, m_new, lo)

    m_ref[...] = jnp.full(m_ref.shape, NEG, _F32)
    acc_ref[...] = jnp.zeros(acc_ref.shape, _F32)
    scores(0, sa_ref)

    def body(t, carry):
        j = sub_blocks * t
        for d in range(sub_blocks):
            nxt, cur = (sb_ref, sa_ref) if d % 2 == 0 else (sa_ref, sb_ref)
            scores(j + d + 1, nxt)
            update(j + d, cur, None)
        return carry

    lax.fori_loop(0, qi, body, 0)
    for d in range(sub_blocks):
        nxt, cur = (sb_ref, sa_ref) if d % 2 == 0 else (sa_ref, sb_ref)
        if d + 1 < sub_blocks:
            scores(sub_blocks * qi + d + 1, nxt, (d + 1) * tk)
        update(sub_blocks * qi + d, cur, d)

    acc = acc_ref[...]
    lp = lam_ref[...]
    lam = (jnp.exp(jnp.sum(lp[0:1] * lp[1:2], axis=1, keepdims=True))
           - jnp.exp(jnp.sum(lp[2:3] * lp[3:4], axis=1, keepdims=True)) + lam_init)
    num, den = acc[:LANES], acc[LANES:LANES + 1]
    ot = num[:, :tq] / den[:, :tq] - lam * (num[:, tq:] / den[:, tq:])
    o = ot.T
    o_ref[...] = _rms(o, sw_ref[...], SUBLN_EPS) * (1.0 - lam_init)


def _diff_call(qk, vt, lam_p, subln_w, layer, batch, seq, tq=DIFF_TQ, tk=DIFF_TK):
    assert tq % tk == 0 and (tq // tk) % 2 == 0
    m = batch * seq
    nq = seq // tq
    nk = seq // tk
    kern = functools.partial(_diff_kernel, tq=tq, tk=tk, lam_init=_lambda_init(layer))
    return pl.pallas_call(
        kern,
        out_shape=jax.ShapeDtypeStruct((m, D_MODEL), _F32),
        grid=(batch, B_HEADS, nq),
        in_specs=[
            pl.BlockSpec((None, 4, HEAD_DIM), lambda b, h, i: (layer, 0, 0)),
            pl.BlockSpec((None, 1, LANES), lambda b, h, i: (layer, 0, 0)),
            pl.BlockSpec((None, tq, LANES), lambda b, h, i: (0, b * nq + i, h)),
            pl.BlockSpec((None, seq, LANES), lambda b, h, i: (1, b, h)),
            pl.BlockSpec((nk, LANES, tk), lambda b, h, i: (b, h, 0)),
        ],
        out_specs=pl.BlockSpec((tq, LANES), lambda b, h, i: (b * nq + i, h)),
        scratch_shapes=[pltpu.VMEM((tk, 2 * tq), _F32), pltpu.VMEM((tk, 2 * tq), _F32),
                        pltpu.VMEM((1, 2 * tq), _F32),
                        pltpu.VMEM((LANES + ONES_ROWS, 2 * tq), _F32)],
        compiler_params=pltpu.CompilerParams(
            dimension_semantics=("parallel", "parallel", "arbitrary"),
            vmem_limit_bytes=VMEM_LIMIT),
        name="diff_attn",
    )(lam_p, subln_w, qk, qk, vt)


def _expand_heads(w, e):
    hi = w.astype(_BF16)
    lo = (w - hi.astype(_F32)).astype(_BF16)
    return (jnp.dot(hi, e, preferred_element_type=_F32)
            + jnp.dot(lo, e, preferred_element_type=_F32))


def _merge_kernel(x_ref, o1_ref, o2_ref, o3_ref, s1_ref, s2_ref, s3_ref, ob_ref, nw_ref,
                  wza_ref, wzb_ref, wga_ref, wgb_ref, wpa_ref, wpb_ref, wo_ref, e_ref, fw_ref,
                  out_ref, on2_ref, on3_ref, sn2_ref, sn3_ref, *, final):
    x = x_ref[...]
    tm = x.shape[0]
    hdn = _rms(x, nw_ref[...], RMS_EPS).astype(_BF16)

    n_blk = D_MODEL // LANES
    for d, src, dst in ((DILATIONS[1], o2_ref, on2_ref), (DILATIONS[2], o3_ref, on3_ref)):
        for r in range(d):
            for blk in range(n_blk):
                dst[blk, pl.ds(r, tm // d, stride=d), :] = src[r, :, blk * LANES:(blk + 1) * LANES]
    for d, src, dst in ((DILATIONS[1], s2_ref, sn2_ref), (DILATIONS[2], s3_ref, sn3_ref)):
        for r in range(d):
            dst[pl.ds(r, tm // d, stride=d), :] = src[r]
    o1 = o1_ref[...]
    outs = ([o1[:, blk * LANES:(blk + 1) * LANES] for blk in range(n_blk)],
            [on2_ref[blk] for blk in range(n_blk)], [on3_ref[blk] for blk in range(n_blk)])

    sts = (s1_ref[...], sn2_ref[...], sn3_ref[...])
    lane = lax.broadcasted_iota(jnp.int32, sts[0].shape, 1)
    mx = jnp.maximum(jnp.maximum(sts[0], sts[1]), sts[2])
    wts = [pltpu.roll(s, LANES - A_HEADS, axis=1) * jnp.exp(s - mx) for s in sts]
    inv = 1.0 / (wts[0] + wts[1] + wts[2])
    e = e_ref[...]
    ya_blocks = None
    for w, o_blocks in zip(wts, outs):
        wn = _expand_heads(jnp.where(lane < A_HEADS, w * inv, 0.0), e)
        terms = [wn[:, blk * LANES:(blk + 1) * LANES] * o_blocks[blk] for blk in range(n_blk)]
        ya_blocks = terms if ya_blocks is None else [a + t for a, t in zip(ya_blocks, terms)]
    ya = jnp.concatenate(ya_blocks, axis=1)

    za = jnp.dot(hdn, wza_ref[...], preferred_element_type=_F32)
    ya = (ya * (za * _sigmoid(za))).astype(_BF16)
    pa = jnp.dot(ya, wpa_ref[...], preferred_element_type=_F32)
    ga = jnp.dot(hdn, wga_ref[...], preferred_element_type=_F32)
    merged = _sigmoid(ga) * pa

    zb = jnp.dot(hdn, wzb_ref[...], preferred_element_type=_F32)
    yb = (ob_ref[...] * (zb * _sigmoid(zb))).astype(_BF16)
    pb = jnp.dot(yb, wpb_ref[...], preferred_element_type=_F32)
    gb = jnp.dot(hdn, wgb_ref[...], preferred_element_type=_F32)
    merged = (merged + _sigmoid(gb) * pb).astype(_BF16)

    y = x + jnp.dot(merged, wo_ref[...], preferred_element_type=_F32)
    if final:
        y = _rms(y, fw_ref[...], RMS_EPS)
    out_ref[...] = y


def _merge_call(x2, os_, sts, ob, norm_w, w_in_bf, wpa_bf, wpb_bf, wo_bf, expand, final_w,
                layer, final, seq, tm=256):
    m = x2.shape[0]
    tiles = seq // tm
    row = lambda width: pl.BlockSpec((tm, width), lambda i: (i, 0))
    pat = lambda d, width: pl.BlockSpec((None, d, tm // d, width),
                                        lambda i: (i // tiles, 0, i % tiles, 0))
    d1, d4, d16 = DILATIONS
    o1 = os_[0].reshape(m, D_MODEL)
    s1 = sts[0].reshape(m, LANES)
    once = pl.Buffered(1)
    w_in_col = lambda c: pl.BlockSpec((None, D_MODEL, D_MODEL), lambda i: (layer, 0, c),
                                      pipeline_mode=once)
    w_sq = pl.BlockSpec((None, D_MODEL, D_MODEL), lambda i: (layer, 0, 0), pipeline_mode=once)
    return pl.pallas_call(
        functools.partial(_merge_kernel, final=final),
        out_shape=jax.ShapeDtypeStruct((m, D_MODEL), _F32),
        grid=(m // tm,),
        in_specs=[
            row(D_MODEL), row(D_MODEL), pat(d4, D_MODEL), pat(d16, D_MODEL),
            row(LANES), pat(d4, LANES), pat(d16, LANES), row(D_MODEL),
            pl.BlockSpec((None, 1, D_MODEL), lambda i: (layer, 0, 0)),
            w_in_col(3), w_in_col(7), w_in_col(8), w_in_col(9),
            w_sq, w_sq, w_sq,
            pl.BlockSpec((LANES, D_MODEL), lambda i: (0, 0), pipeline_mode=once),
            pl.BlockSpec((1, D_MODEL), lambda i: (0, 0)),
        ],
        out_specs=row(D_MODEL),
        scratch_shapes=[pltpu.VMEM((D_MODEL // LANES, tm, LANES), _F32)] * 2
                       + [pltpu.VMEM((tm, LANES), _F32)] * 2,
        compiler_params=pltpu.CompilerParams(
            dimension_semantics=("parallel",), vmem_limit_bytes=VMEM_LIMIT),
        name="merge_out",
    )(x2, o1, os_[1], os_[2], s1, sts[1], sts[2], ob, norm_w, w_in_bf, w_in_bf, w_in_bf, w_in_bf,
      wpa_bf, wpb_bf, wo_bf, expand, final_w)


def _rope_tables(seq):
    half = ROPE_DIM // 2
    inv = 1.0 / (ROPE_THETA ** (jnp.arange(0, ROPE_DIM, 2, dtype=_F32) / ROPE_DIM))
    ang = jnp.arange(seq, dtype=_F32)[:, None] * inv[None, :]
    cos, sin = jnp.cos(ang), jnp.sin(ang)
    rest = HEAD_DIM - ROPE_DIM
    z = lambda n: jnp.zeros((seq, n), _F32)
    cos_h = jnp.concatenate([cos, cos, jnp.ones((seq, rest), _F32)], axis=1)
    sa_h = jnp.concatenate([z(half), sin, z(rest)], axis=1)
    sb_h = jnp.concatenate([-sin, z(half), z(rest)], axis=1)
    rep = LANES // HEAD_DIM
    return tuple(jnp.tile(t, (1, rep)) for t in (cos_h, sa_h, sb_h))


def kernel(x, norm_w, w_in, lambda_q1, lambda_k1, lambda_q2, lambda_k2, subln_w,
           w_proj_a, w_proj_b, w_out, final_norm_w):
    batch, seq, d = x.shape
    depth = norm_w.shape[0]
    assert d == D_MODEL and seq % (BAND * DILATIONS[-1]) == 0
    assert w_in.shape[-1] == N_COL_BLOCKS * D_MODEL

    w_in_bf = w_in.astype(_BF16)
    wpa_bf = w_proj_a.astype(_BF16)
    wpb_bf = w_proj_b.astype(_BF16)
    wo_bf = w_out.astype(_BF16)
    lam_p = jnp.stack([lambda_q1, lambda_k1, lambda_q2, lambda_k2], axis=1)
    tabs = _rope_tables(seq)
    bias = _band_bias()
    head_of_col = jnp.arange(D_MODEL)[None, :] // HEAD_DIM
    expand = (jnp.arange(LANES)[:, None] == head_of_col).astype(_BF16)
    final_w = final_norm_w.reshape(1, D_MODEL)
    norm_w = norm_w.reshape(depth, 1, D_MODEL)
    subln_w = subln_w.reshape(depth, 1, LANES)

    h = x.reshape(batch * seq, D_MODEL)
    for layer in range(depth):
        *a_layouts, qk, vt = _proj_call(h, norm_w, w_in_bf, layer, tabs, batch, seq)
        os_, sts = zip(*[_dilated_call(a, bias, dil, batch, seq)
                         for a, dil in zip(a_layouts, DILATIONS)])
        ob = _diff_call(qk, vt, lam_p, subln_w, layer, batch, seq)
        h = _merge_call(h, os_, sts, ob, norm_w, w_in_bf, wpa_bf, wpb_bf, wo_bf, expand,
                        final_w, layer, layer == depth - 1, seq)
    return h.reshape(batch, seq, D_MODEL)
```

```python
import functools
import math

import jax
import jax.numpy as jnp
from jax import lax
from jax.experimental import pallas as pl
from jax.experimental.pallas import tpu as pltpu

D_MODEL = 1024
HEAD_DIM = 64
A_HEADS = 16
B_HEADS = 8
DILATIONS = (1, 4, 16)
BAND = 128
ROPE_THETA = 500000.0
ROPE_DIM = HEAD_DIM // 4
RMS_EPS = 1e-6
SUBLN_EPS = 1e-5
NEG = -1e30
LANES = 128
N_COL_BLOCKS = 10
VMEM_LIMIT = 56 * 1024 * 1024
DIFF_TQ = 2048
DIL_QB = 4
DIFF_TK = 512
ONES_ROWS = 16

_F32 = jnp.float32
_BF16 = jnp.bfloat16


def _lambda_init(layer):
    return 0.8 - 0.6 * math.exp(-0.3 * layer)


def _rms(x, w, eps):
    return x * lax.rsqrt(jnp.mean(x * x, axis=-1, keepdims=True) + eps) * w


def _sigmoid(z):
    return 1.0 / (1.0 + jnp.exp(-z))


def _proj_kernel(x_ref, nw_ref, wqa_ref, wka_ref, wva_ref, wqb_ref, wkb_ref, wvb_ref,
                 cos_ref, sa_ref, sb_ref, a1_ref, a4_ref, a16_ref, qk_ref, vt_ref,
                 scr0_ref, scr1_ref):
    tm = x_ref.shape[0]
    hdn = _rms(x_ref[...], nw_ref[...], RMS_EPS).astype(_BF16)
    cos, sa, sb = cos_ref[...], sa_ref[...], sb_ref[...]
    q_scale = HEAD_DIM ** -0.5
    tabs = {True: (cos * q_scale, sa * q_scale, sb * q_scale), False: (cos, sa, sb)}

    def lane_blocks(u):
        return [u[:, blk * LANES:(blk + 1) * LANES] for blk in range(D_MODEL // LANES)]

    def rope(u, is_q):
        c, s_a, s_b = tabs[is_q]
        return [t * c + pltpu.roll(t, ROPE_DIM // 2, axis=1) * s_a
                + pltpu.roll(t, LANES - ROPE_DIM // 2, axis=1) * s_b for t in lane_blocks(u)]

    d4, d16 = DILATIONS[1], DILATIONS[2]
    step = d16 // d4

    def residue_layouts(blocks, which):
        for blk, t in enumerate(blocks):
            sl = slice(blk * LANES, (blk + 1) * LANES)
            a1_ref[which, :, sl] = t.astype(_BF16)
            scr0_ref[blk] = t
            for c in range(d4):
                x = scr0_ref[blk, pl.ds(c, tm // d4, stride=d4), :]
                a4_ref[which, c, :, sl] = x.astype(_BF16)
                scr1_ref[blk, c * (tm // d4):(c + 1) * (tm // d4), :] = x
            for r in range(d16):
                c, b = r % d4, r // d4
                x = scr1_ref[blk, pl.ds(c * (tm // d4) + b, tm // d16, stride=step), :]
                a16_ref[which, r, :, sl] = x.astype(_BF16)

    def natural(ref, blocks, which):
        for blk, t in enumerate(blocks):
            ref[which, :, blk * LANES:(blk + 1) * LANES] = t.astype(_BF16)

    dot = lambda w_ref: jnp.dot(hdn, w_ref[...], preferred_element_type=_F32)
    residue_layouts(rope(dot(wqa_ref), True), 0)
    residue_layouts(rope(dot(wka_ref), False), 1)
    residue_layouts(lane_blocks(dot(wva_ref)), 2)
    natural(qk_ref, rope(dot(wqb_ref), True), 0)
    natural(qk_ref, rope(dot(wkb_ref), False), 1)
    vt_ref[0] = dot(wvb_ref).T.astype(_BF16)


def _proj_call(x2, norm_w, w_in_bf, layer, tabs, batch, seq, tm=DIFF_TK):
    m = x2.shape[0]
    tiles = seq // tm
    cos_t, sa_t, sb_t = tabs
    once = pl.Buffered(1)
    w_col = lambda c: pl.BlockSpec((None, D_MODEL, D_MODEL), lambda i: (layer, 0, c),
                                   pipeline_mode=once)
    tab_spec = pl.BlockSpec((tm, LANES), lambda i: (i % tiles, 0))
    a_shape = lambda d: jax.ShapeDtypeStruct((3, batch, d, seq // d, D_MODEL), _BF16)
    a_spec = lambda d: pl.BlockSpec((3, None, d, tm // d, D_MODEL),
                                    lambda i: (0, i // tiles, 0, i % tiles, 0))
    d1, d4, d16 = DILATIONS
    return pl.pallas_call(
        _proj_kernel,
        out_shape=(a_shape(d1), a_shape(d4), a_shape(d16),
                   jax.ShapeDtypeStruct((2, m, D_MODEL), _BF16),
                   jax.ShapeDtypeStruct((m // tm, D_MODEL, tm), _BF16)),
        grid=(m // tm,),
        in_specs=[
            pl.BlockSpec((tm, D_MODEL), lambda i: (i, 0)),
            pl.BlockSpec((None, 1, D_MODEL), lambda i: (layer, 0, 0)),
            w_col(0), w_col(1), w_col(2), w_col(4), w_col(5), w_col(6),
            tab_spec, tab_spec, tab_spec,
        ],
        out_specs=(pl.BlockSpec((3, None, None, tm, D_MODEL),
                                lambda i: (0, i // tiles, 0, i % tiles, 0)),
                   a_spec(d4), a_spec(d16),
                   pl.BlockSpec((2, tm, D_MODEL), lambda i: (0, i, 0)),
                   pl.BlockSpec((1, D_MODEL, tm), lambda i: (i, 0, 0))),
        scratch_shapes=[pltpu.VMEM((D_MODEL // LANES, tm, LANES), _F32)] * 2,
        compiler_params=pltpu.CompilerParams(
            dimension_semantics=("parallel",), vmem_limit_bytes=VMEM_LIMIT),
        name="qkv_proj",
    )(x2, norm_w, w_in_bf, w_in_bf, w_in_bf, w_in_bf, w_in_bf, w_in_bf, cos_t, sa_t, sb_t)


def _dilated_kernel(bias0_ref, bias_ref, q_ref, kp_ref, kc_ref, vp_ref, vc_ref, o_ref, st_ref):
    lane = lax.broadcasted_iota(jnp.int32, (BAND, LANES), 1)
    low = lane < HEAD_DIM
    ones = jnp.ones((2 * BAND, LANES), _BF16)
    for u in range(DIL_QB):
        rows = slice(u * BAND, (u + 1) * BAND)
        prev = slice((u - 1) * BAND, u * BAND)
        bias = (bias0_ref if u == 0 else bias_ref)[...]
        st = jnp.zeros((BAND, LANES), _F32)
        for hp in range(A_HEADS // 2):
            sl = slice(hp * LANES, (hp + 1) * LANES)
            q = q_ref[rows, sl]
            zero = jnp.zeros_like(q)
            qq = jnp.concatenate([jnp.where(low, q, zero), jnp.where(low, zero, q)], axis=0)
            k_prev = kp_ref[:, sl] if u == 0 else kc_ref[prev, sl]
            v_prev = vp_ref[:, sl] if u == 0 else vc_ref[prev, sl]
            kk = jnp.concatenate([k_prev, kc_ref[rows, sl]], axis=0)
            va = jnp.concatenate([jnp.concatenate([v_prev, vc_ref[rows, sl]], axis=0), ones],
                                 axis=1)
            s = lax.dot_general(qq, kk, (((1,), (1,)), ((), ())),
                                preferred_element_type=_F32) + bias
            m = jnp.max(s, axis=1, keepdims=True)
            p = jnp.exp(s - m).astype(_BF16)
            pv = jnp.dot(p, va, preferred_element_type=_F32)
            num, den = pv[:, :LANES], pv[:, LANES:]
            o = num / den
            o_ref[rows, sl] = jnp.where(low, o[:BAND], o[BAND:])
            for e in range(2):
                h = 2 * hp + e
                st = jnp.where(lane == h, m[e * BAND:(e + 1) * BAND], st)
                st = jnp.where(lane == A_HEADS + h, den[e * BAND:(e + 1) * BAND], st)
        st_ref[rows, :] = st


def _dilated_call(qkv, bias, dil, batch, seq):
    per_class = seq // dil
    rows = DIL_QB * BAND
    steps = per_class // rows
    cur = lambda which: pl.BlockSpec((None, None, None, rows, D_MODEL),
                                     lambda b, r, n: (which, b, r, n, 0))
    prev = lambda which: pl.BlockSpec(
        (None, None, None, BAND, D_MODEL),
        lambda b, r, n: (which, b, r, jnp.maximum(n * DIL_QB - 1, 0), 0))
    return pl.pallas_call(
        _dilated_kernel,
        out_shape=(jax.ShapeDtypeStruct((batch, dil, per_class, D_MODEL), _F32),
                   jax.ShapeDtypeStruct((batch, dil, per_class, LANES), _F32)),
        grid=(batch, dil, steps),
        in_specs=[
            pl.BlockSpec((None, 2 * BAND, 2 * BAND), lambda b, r, n: (jnp.minimum(n, 1), 0, 0)),
            pl.BlockSpec((None, 2 * BAND, 2 * BAND), lambda b, r, n: (1, 0, 0)),
            cur(0), prev(1), cur(1), prev(2), cur(2),
        ],
        out_specs=(pl.BlockSpec((None, None, rows, D_MODEL), lambda b, r, n: (b, r, n, 0)),
                   pl.BlockSpec((None, None, rows, LANES), lambda b, r, n: (b, r, n, 0))),
        compiler_params=pltpu.CompilerParams(
            dimension_semantics=("parallel", "parallel", "arbitrary"),
            vmem_limit_bytes=VMEM_LIMIT),
        name=f"dilated_d{dil}",
    )(bias, bias, qkv, qkv, qkv, qkv, qkv)


def _band_bias():
    qi = jnp.arange(BAND)[:, None]
    kc = jnp.arange(2 * BAND)[None, :]
    ok = (kc >= qi) & (kc <= qi + BAND)
    first = ok & (kc >= BAND)
    both = jnp.stack([first, ok]).astype(_F32)
    both = jnp.concatenate([both, both], axis=1)
    return (1.0 - both) * NEG


def _diff_kernel(lam_ref, sw_ref, q_ref, k_ref, vt_ref, o_ref, sa_ref, sb_ref, m_ref, acc_ref,
                 *, tq, tk, lam_init):
    qi = pl.program_id(2)
    lane = lax.broadcasted_iota(jnp.int32, (tq, LANES), 1)
    low = lane < HEAD_DIM
    q = q_ref[...]
    zero = jnp.zeros_like(q)
    qq = jnp.concatenate([jnp.where(low, q, zero), jnp.where(low, zero, q)], axis=0)
    ones = jnp.ones((ONES_ROWS, tk), _BF16)
    sub_blocks = tq // tk

    def live(x, lo, axis):
        if lo == 0:
            return x
        take = lambda a, b: lax.slice_in_dim(x, a, b, axis=axis)
        return jnp.concatenate([take(lo, tq), take(tq + lo, 2 * tq)], axis=axis)

    def put(ref, val, lo):
        if lo == 0:
            ref[...] = val
        else:
            ref[:, lo:tq] = val[:, :tq - lo]
            ref[:, tq + lo:] = val[:, tq - lo:]

    def scores(j, buf, lo=0):
        start = j * tk if isinstance(j, int) else pl.multiple_of(j * tk, tk)
        kb = k_ref[pl.ds(start, tk), :]
        buf[:, :2 * (tq - lo)] = lax.dot_general(kb, live(qq, lo, 0), (((1,), (1,)), ((), ())),
                                                 preferred_element_type=_F32)
    def update(j, buf, diag):
        lo = 0 if diag is None else diag * tk
        n = tq - lo
        s = buf[:, :2 * n]
        if diag is not None:
            key = lax.broadcasted_iota(jnp.int32, (tk, 2 * n), 0)
            qry = lax.broadcasted_iota(jnp.int32, (tk, 2 * n), 1)
            qry = jnp.where(qry >= n, qry - n, qry)
            s = jnp.where(key <= qry, s, NEG)
        m = live(m_ref[...], lo, 1)
        m_new = jnp.maximum(m, jnp.max(s, axis=0, keepdims=True))
        alpha = jnp.exp(m - m_new)
        p = jnp.exp(s - m_new).astype(_BF16)
        va = jnp.concatenate([vt_ref[j], ones], axis=0)
        acc = live(acc_ref[...], lo, 1)
        put(acc_ref, alpha * acc + jnp.dot(va, p, preferred_element_type=_F32), lo)
        put(m_ref, m_new, lo)

    m_ref[...] = jnp.full(m_ref.shape, NEG, _F32)
    acc_ref[...] = jnp.zeros(acc_ref.shape, _F32)
    scores(0, sa_ref)

    def body(t, carry):
        j = sub_blocks * t
        for d in range(sub_blocks):
            nxt, cur = (sb_ref, sa_ref) if d % 2 == 0 else (sa_ref, sb_ref)
            scores(j + d + 1, nxt)
            update(j + d, cur, None)
        return carry

    lax.fori_loop(0, qi, body, 0)
    for d in range(sub_blocks):
        nxt, cur = (sb_ref, sa_ref) if d % 2 == 0 else (sa_ref, sb_ref)
        if d + 1 < sub_blocks:
            scores(sub_blocks * qi + d + 1, nxt, (d + 1) * tk)
        update(sub_blocks * qi + d, cur, d)

    acc = acc_ref[...]
    lp = lam_ref[...]
    lam = (jnp.exp(jnp.sum(lp[0:1] * lp[1:2], axis=1, keepdims=True))
           - jnp.exp(jnp.sum(lp[2:3] * lp[3:4], axis=1, keepdims=True)) + lam_init)
    num, den = acc[:LANES], acc[LANES:LANES + 1]
    ot = num[:, :tq] / den[:, :tq] - lam * (num[:, tq:] / den[:, tq:])
    o = ot.T
    o_ref[...] = _rms(o, sw_ref[...], SUBLN_EPS) * (1.0 - lam_init)


def _diff_call(qk, vt, lam_p, subln_w, layer, batch, seq, tq=DIFF_TQ, tk=DIFF_TK):
    assert tq % tk == 0 and (tq // tk) % 2 == 0
    m = batch * seq
    nq = seq // tq
    nk = seq // tk
    kern = functools.partial(_diff_kernel, tq=tq, tk=tk, lam_init=_lambda_init(layer))
    return pl.pallas_call(
        kern,
        out_shape=jax.ShapeDtypeStruct((m, D_MODEL), _F32),
        grid=(batch, B_HEADS, nq),
        in_specs=[
            pl.BlockSpec((None, 4, HEAD_DIM), lambda b, h, i: (layer, 0, 0)),
            pl.BlockSpec((None, 1, LANES), lambda b, h, i: (layer, 0, 0)),
            pl.BlockSpec((None, tq, LANES), lambda b, h, i: (0, b * nq + i, h)),
            pl.BlockSpec((None, seq, LANES), lambda b, h, i: (1, b, h)),
            pl.BlockSpec((nk, LANES, tk), lambda b, h, i: (b, h, 0)),
        ],
        out_specs=pl.BlockSpec((tq, LANES), lambda b, h, i: (b * nq + i, h)),
        scratch_shapes=[pltpu.VMEM((tk, 2 * tq), _F32), pltpu.VMEM((tk, 2 * tq), _F32),
                        pltpu.VMEM((1, 2 * tq), _F32),
                        pltpu.VMEM((LANES + ONES_ROWS, 2 * tq), _F32)],
        compiler_params=pltpu.CompilerParams(
            dimension_semantics=("parallel", "parallel", "arbitrary"),
            vmem_limit_bytes=VMEM_LIMIT),
        name="diff_attn",
    )(lam_p, subln_w, qk, qk, vt)


def _expand_heads(w, e):
    hi = w.astype(_BF16)
    lo = (w - hi.astype(_F32)).astype(_BF16)
    return (jnp.dot(hi, e, preferred_element_type=_F32)
            + jnp.dot(lo, e, preferred_element_type=_F32))


def _merge_kernel(x_ref, o1_ref, o2_ref, o3_ref, s1_ref, s2_ref, s3_ref, ob_ref, nw_ref,
                  wza_ref, wzb_ref, wga_ref, wgb_ref, wpa_ref, wpb_ref, wo_ref, e_ref, fw_ref,
                  out_ref, on2_ref, on3_ref, sn2_ref, sn3_ref, *, final):
    x = x_ref[...]
    tm = x.shape[0]
    hdn = _rms(x, nw_ref[...], RMS_EPS).astype(_BF16)

    n_blk = D_MODEL // LANES
    for d, src, dst in ((DILATIONS[1], o2_ref, on2_ref), (DILATIONS[2], o3_ref, on3_ref)):
        for r in range(d):
            for blk in range(n_blk):
                dst[blk, pl.ds(r, tm // d, stride=d), :] = src[r, :, blk * LANES:(blk + 1) * LANES]
    for d, src, dst in ((DILATIONS[1], s2_ref, sn2_ref), (DILATIONS[2], s3_ref, sn3_ref)):
        for r in range(d):
            dst[pl.ds(r, tm // d, stride=d), :] = src[r]
    o1 = o1_ref[...]
    outs = ([o1[:, blk * LANES:(blk + 1) * LANES] for blk in range(n_blk)],
            [on2_ref[blk] for blk in range(n_blk)], [on3_ref[blk] for blk in range(n_blk)])

    sts = (s1_ref[...], sn2_ref[...], sn3_ref[...])
    lane = lax.broadcasted_iota(jnp.int32, sts[0].shape, 1)
    mx = jnp.maximum(jnp.maximum(sts[0], sts[1]), sts[2])
    wts = [pltpu.roll(s, LANES - A_HEADS, axis=1) * jnp.exp(s - mx) for s in sts]
    inv = 1.0 / (wts[0] + wts[1] + wts[2])
    e = e_ref[...]
    ya_blocks = None
    for w, o_blocks in zip(wts, outs):
        wn = _expand_heads(jnp.where(lane < A_HEADS, w * inv, 0.0), e)
        terms = [wn[:, blk * LANES:(blk + 1) * LANES] * o_blocks[blk] for blk in range(n_blk)]
        ya_blocks = terms if ya_blocks is None else [a + t for a, t in zip(ya_blocks, terms)]
    ya = jnp.concatenate(ya_blocks, axis=1)

    za = jnp.dot(hdn, wza_ref[...], preferred_element_type=_F32)
    ya = (ya * (za * _sigmoid(za))).astype(_BF16)
    pa = jnp.dot(ya, wpa_ref[...], preferred_element_type=_F32)
    ga = jnp.dot(hdn, wga_ref[...], preferred_element_type=_F32)
    merged = _sigmoid(ga) * pa

    zb = jnp.dot(hdn, wzb_ref[...], preferred_element_type=_F32)
    yb = (ob_ref[...] * (zb * _sigmoid(zb))).astype(_BF16)
    pb = jnp.dot(yb, wpb_ref[...], preferred_element_type=_F32)
    gb = jnp.dot(hdn, wgb_ref[...], preferred_element_type=_F32)
    merged = (merged + _sigmoid(gb) * pb).astype(_BF16)

    y = x + jnp.dot(merged, wo_ref[...], preferred_element_type=_F32)
    if final:
        y = _rms(y, fw_ref[...], RMS_EPS)
    out_ref[...] = y


def _merge_call(x2, os_, sts, ob, norm_w, w_in_bf, wpa_bf, wpb_bf, wo_bf, expand, final_w,
                layer, final, seq, tm=256):
    m = x2.shape[0]
    tiles = seq // tm
    row = lambda width: pl.BlockSpec((tm, width), lambda i: (i, 0))
    pat = lambda d, width: pl.BlockSpec((None, d, tm // d, width),
                                        lambda i: (i // tiles, 0, i % tiles, 0))
    d1, d4, d16 = DILATIONS
    o1 = os_[0].reshape(m, D_MODEL)
    s1 = sts[0].reshape(m, LANES)
    once = pl.Buffered(1)
    w_in_col = lambda c: pl.BlockSpec((None, D_MODEL, D_MODEL), lambda i: (layer, 0, c),
                                      pipeline_mode=once)
    w_sq = pl.BlockSpec((None, D_MODEL, D_MODEL), lambda i: (layer, 0, 0), pipeline_mode=once)
    return pl.pallas_call(
        functools.partial(_merge_kernel, final=final),
        out_shape=jax.ShapeDtypeStruct((m, D_MODEL), _F32),
        grid=(m // tm,),
        in_specs=[
            row(D_MODEL), row(D_MODEL), pat(d4, D_MODEL), pat(d16, D_MODEL),
            row(LANES), pat(d4, LANES), pat(d16, LANES), row(D_MODEL),
            pl.BlockSpec((None, 1, D_MODEL), lambda i: (layer, 0, 0)),
            w_in_col(3), w_in_col(7), w_in_col(8), w_in_col(9),
            w_sq, w_sq, w_sq,
            pl.BlockSpec((LANES, D_MODEL), lambda i: (0, 0), pipeline_mode=once),
            pl.BlockSpec((1, D_MODEL), lambda i: (0, 0)),
        ],
        out_specs=row(D_MODEL),
        scratch_shapes=[pltpu.VMEM((D_MODEL // LANES, tm, LANES), _F32)] * 2
                       + [pltpu.VMEM((tm, LANES), _F32)] * 2,
        compiler_params=pltpu.CompilerParams(
            dimension_semantics=("parallel",), vmem_limit_bytes=VMEM_LIMIT),
        name="merge_out",
    )(x2, o1, os_[1], os_[2], s1, sts[1], sts[2], ob, norm_w, w_in_bf, w_in_bf, w_in_bf, w_in_bf,
      wpa_bf, wpb_bf, wo_bf, expand, final_w)


def _rope_tables(seq):
    half = ROPE_DIM // 2
    inv = 1.0 / (ROPE_THETA ** (jnp.arange(0, ROPE_DIM, 2, dtype=_F32) / ROPE_DIM))
    ang = jnp.arange(seq, dtype=_F32)[:, None] * inv[None, :]
    cos, sin = jnp.cos(ang), jnp.sin(ang)
    rest = HEAD_DIM - ROPE_DIM
    z = lambda n: jnp.zeros((seq, n), _F32)
    cos_h = jnp.concatenate([cos, cos, jnp.ones((seq, rest), _F32)], axis=1)
    sa_h = jnp.concatenate([z(half), sin, z(rest)], axis=1)
    sb_h = jnp.concatenate([-sin, z(half), z(rest)], axis=1)
    rep = LANES // HEAD_DIM
    return tuple(jnp.tile(t, (1, rep)) for t in (cos_h, sa_h, sb_h))


def kernel(x, norm_w, w_in, lambda_q1, lambda_k1, lambda_q2, lambda_k2, subln_w,
           w_proj_a, w_proj_b, w_out, final_norm_w):
    batch, seq, d = x.shape
    depth = norm_w.shape[0]
    assert d == D_MODEL and seq % (BAND * DILATIONS[-1]) == 0
    assert w_in.shape[-1] == N_COL_BLOCKS * D_MODEL

    w_in_bf = w_in.astype(_BF16)
    wpa_bf = w_proj_a.astype(_BF16)
    wpb_bf = w_proj_b.astype(_BF16)
    wo_bf = w_out.astype(_BF16)
    lam_p = jnp.stack([lambda_q1, lambda_k1, lambda_q2, lambda_k2], axis=1)
    tabs = _rope_tables(seq)
    bias = _band_bias()
    head_of_col = jnp.arange(D_MODEL)[None, :] // HEAD_DIM
    expand = (jnp.arange(LANES)[:, None] == head_of_col).astype(_BF16)
    final_w = final_norm_w.reshape(1, D_MODEL)
    norm_w = norm_w.reshape(depth, 1, D_MODEL)
    subln_w = subln_w.reshape(depth, 1, LANES)

    h = x.reshape(batch * seq, D_MODEL)
    for layer in range(depth):
        *a_layouts, qk, vt = _proj_call(h, norm_w, w_in_bf, layer, tabs, batch, seq)
        os_, sts = zip(*[_dilated_call(a, bias, dil, batch, seq)
                         for a, dil in zip(a_layouts, DILATIONS)])
        ob = _diff_call(qk, vt, lam_p, subln_w, layer, batch, seq)
        h = _merge_call(h, os_, sts, ob, norm_w, w_in_bf, wpa_bf, wpb_bf, wo_bf, expand,
                        final_w, layer, layer == depth - 1, seq)
    return h.reshape(batch, seq, D_MODEL)
```

```python
import functools
import math

import jax
import jax.numpy as jnp
from jax import lax
from jax.experimental import pallas as pl
from jax.experimental.pallas import tpu as pltpu

D_MODEL = 1024
HEAD_DIM = 64
A_HEADS = 16
B_HEADS = 8
DILATIONS = (1, 4, 16)
BAND = 128
ROPE_THETA = 500000.0
ROPE_DIM = HEAD_DIM // 4
RMS_EPS = 1e-6
SUBLN_EPS = 1e-5
NEG = -1e30
LANES = 128
N_COL_BLOCKS = 10
VMEM_LIMIT = 56 * 1024 * 1024
DIFF_TQ = 2048
DIL_QB = 4
DIFF_TK = 512
DIFF_CHUNK = 256
ONES_ROWS = 16

_F32 = jnp.float32
_BF16 = jnp.bfloat16


def _lambda_init(layer):
    return 0.8 - 0.6 * math.exp(-0.3 * layer)


def _rms(x, w, eps):
    return x * lax.rsqrt(jnp.mean(x * x, axis=-1, keepdims=True) + eps) * w


def _sigmoid(z):
    return 1.0 / (1.0 + jnp.exp(-z))


def _proj_kernel(x_ref, nw_ref, wqa_ref, wka_ref, wva_ref, wqb_ref, wkb_ref, wvb_ref,
                 cos_ref, sa_ref, sb_ref, a1_ref, a4_ref, a16_ref, qk_ref, vt_ref,
                 scr0_ref, scr1_ref):
    tm = x_ref.shape[0]
    hdn = _rms(x_ref[...], nw_ref[...], RMS_EPS).astype(_BF16)
    cos, sa, sb = cos_ref[...], sa_ref[...], sb_ref[...]
    q_scale = HEAD_DIM ** -0.5
    tabs = {True: (cos * q_scale, sa * q_scale, sb * q_scale), False: (cos, sa, sb)}

    def lane_blocks(u):
        return [u[:, blk * LANES:(blk + 1) * LANES] for blk in range(D_MODEL // LANES)]

    def rope(u, is_q):
        c, s_a, s_b = tabs[is_q]
        return [t * c + pltpu.roll(t, ROPE_DIM // 2, axis=1) * s_a
                + pltpu.roll(t, LANES - ROPE_DIM // 2, axis=1) * s_b for t in lane_blocks(u)]

    d4, d16 = DILATIONS[1], DILATIONS[2]
    step = d16 // d4

    def residue_layouts(blocks, which):
        for blk, t in enumerate(blocks):
            sl = slice(blk * LANES, (blk + 1) * LANES)
            a1_ref[which, :, sl] = t.astype(_BF16)
            scr0_ref[blk] = t
            for c in range(d4):
                x = scr0_ref[blk, pl.ds(c, tm // d4, stride=d4), :]
                a4_ref[which, c, :, sl] = x.astype(_BF16)
                scr1_ref[blk, c * (tm // d4):(c + 1) * (tm // d4), :] = x
            for r in range(d16):
                c, b = r % d4, r // d4
                x = scr1_ref[blk, pl.ds(c * (tm // d4) + b, tm // d16, stride=step), :]
                a16_ref[which, r, :, sl] = x.astype(_BF16)

    def natural(ref, blocks, which):
        for blk, t in enumerate(blocks):
            ref[which, :, blk * LANES:(blk + 1) * LANES] = t.astype(_BF16)

    dot = lambda w_ref: jnp.dot(hdn, w_ref[...], preferred_element_type=_F32)
    residue_layouts(rope(dot(wqa_ref), True), 0)
    residue_layouts(rope(dot(wka_ref), False), 1)
    residue_layouts(lane_blocks(dot(wva_ref)), 2)
    natural(qk_ref, rope(dot(wqb_ref), True), 0)
    natural(qk_ref, rope(dot(wkb_ref), False), 1)
    vt_ref[0] = dot(wvb_ref).T.astype(_BF16)


def _proj_call(x2, norm_w, w_in_bf, layer, tabs, batch, seq, tm=DIFF_TK):
    m = x2.shape[0]
    tiles = seq // tm
    cos_t, sa_t, sb_t = tabs
    once = pl.Buffered(1)
    w_col = lambda c: pl.BlockSpec((None, D_MODEL, D_MODEL), lambda i: (layer, 0, c),
                                   pipeline_mode=once)
    tab_spec = pl.BlockSpec((tm, LANES), lambda i: (i % tiles, 0))
    a_shape = lambda d: jax.ShapeDtypeStruct((3, batch, d, seq // d, D_MODEL), _BF16)
    a_spec = lambda d: pl.BlockSpec((3, None, d, tm // d, D_MODEL),
                                    lambda i: (0, i // tiles, 0, i % tiles, 0))
    d1, d4, d16 = DILATIONS
    return pl.pallas_call(
        _proj_kernel,
        out_shape=(a_shape(d1), a_shape(d4), a_shape(d16),
                   jax.ShapeDtypeStruct((2, m, D_MODEL), _BF16),
                   jax.ShapeDtypeStruct((m // tm, D_MODEL, tm), _BF16)),
        grid=(m // tm,),
        in_specs=[
            pl.BlockSpec((tm, D_MODEL), lambda i: (i, 0)),
            pl.BlockSpec((None, 1, D_MODEL), lambda i: (layer, 0, 0)),
            w_col(0), w_col(1), w_col(2), w_col(4), w_col(5), w_col(6),
            tab_spec, tab_spec, tab_spec,
        ],
        out_specs=(pl.BlockSpec((3, None, None, tm, D_MODEL),
                                lambda i: (0, i // tiles, 0, i % tiles, 0)),
                   a_spec(d4), a_spec(d16),
                   pl.BlockSpec((2, tm, D_MODEL), lambda i: (0, i, 0)),
                   pl.BlockSpec((1, D_MODEL, tm), lambda i: (i, 0, 0))),
        scratch_shapes=[pltpu.VMEM((D_MODEL // LANES, tm, LANES), _F32)] * 2,
        compiler_params=pltpu.CompilerParams(
            dimension_semantics=("parallel",), vmem_limit_bytes=VMEM_LIMIT),
        name="qkv_proj",
    )(x2, norm_w, w_in_bf, w_in_bf, w_in_bf, w_in_bf, w_in_bf, w_in_bf, cos_t, sa_t, sb_t)


def _dilated_kernel(bias0_ref, bias_ref, q_ref, kp_ref, kc_ref, vp_ref, vc_ref, o_ref, st_ref):
    lane = lax.broadcasted_iota(jnp.int32, (BAND, LANES), 1)
    low = lane < HEAD_DIM
    ones = jnp.ones((2 * BAND, LANES), _BF16)

    def band_scores(u, hp):
        rows = slice(u * BAND, (u + 1) * BAND)
        sl = slice(hp * LANES, (hp + 1) * LANES)
        q = q_ref[rows, sl]
        zero = jnp.zeros_like(q)
        qq = jnp.concatenate([jnp.where(low, q, zero), jnp.where(low, zero, q)], axis=0)
        k_prev = kp_ref[:, sl] if u == 0 else kc_ref[(u - 1) * BAND:u * BAND, sl]
        kk = jnp.concatenate([k_prev, kc_ref[rows, sl]], axis=0)
        bias = (bias0_ref if u == 0 else bias_ref)[...]
        return lax.dot_general(qq, kk, (((1,), (1,)), ((), ())),
                               preferred_element_type=_F32) + bias

    work = [(u, hp) for u in range(DIL_QB) for hp in range(A_HEADS // 2)]
    s = band_scores(*work[0])
    st = None
    for i, (u, hp) in enumerate(work):
        s_next = band_scores(*work[i + 1]) if i + 1 < len(work) else None
        rows = slice(u * BAND, (u + 1) * BAND)
        sl = slice(hp * LANES, (hp + 1) * LANES)
        if hp == 0:
            st = jnp.zeros((BAND, LANES), _F32)
        v_prev = vp_ref[:, sl] if u == 0 else vc_ref[(u - 1) * BAND:u * BAND, sl]
        va = jnp.concatenate([jnp.concatenate([v_prev, vc_ref[rows, sl]], axis=0), ones],
                             axis=1)
        m = jnp.max(s, axis=1, keepdims=True)
        p = jnp.exp(s - m).astype(_BF16)
        pv = jnp.dot(p, va, preferred_element_type=_F32)
        num, den = pv[:, :LANES], pv[:, LANES:]
        o_ref[rows, sl] = jnp.where(low, num[:BAND], num[BAND:])
        for e in range(2):
            h = 2 * hp + e
            st = jnp.where(lane == h, m[e * BAND:(e + 1) * BAND], st)
            st = jnp.where(lane == A_HEADS + h, den[e * BAND:(e + 1) * BAND], st)
        if hp == A_HEADS // 2 - 1:
            st_ref[rows, :] = st
        s = s_next


def _dilated_call(qkv, bias, dil, batch, seq):
    per_class = seq // dil
    rows = DIL_QB * BAND
    steps = per_class // rows
    cur = lambda which: pl.BlockSpec((None, None, None, rows, D_MODEL),
                                     lambda b, r, n: (which, b, r, n, 0))
    prev = lambda which: pl.BlockSpec(
        (None, None, None, BAND, D_MODEL),
        lambda b, r, n: (which, b, r, jnp.maximum(n * DIL_QB - 1, 0), 0))
    return pl.pallas_call(
        _dilated_kernel,
        out_shape=(jax.ShapeDtypeStruct((batch, dil, per_class, D_MODEL), _F32),
                   jax.ShapeDtypeStruct((batch, dil, per_class, LANES), _F32)),
        grid=(batch, dil, steps),
        in_specs=[
            pl.BlockSpec((None, 2 * BAND, 2 * BAND), lambda b, r, n: (jnp.minimum(n, 1), 0, 0)),
            pl.BlockSpec((None, 2 * BAND, 2 * BAND), lambda b, r, n: (1, 0, 0)),
            cur(0), prev(1), cur(1), prev(2), cur(2),
        ],
        out_specs=(pl.BlockSpec((None, None, rows, D_MODEL), lambda b, r, n: (b, r, n, 0)),
                   pl.BlockSpec((None, None, rows, LANES), lambda b, r, n: (b, r, n, 0))),
        compiler_params=pltpu.CompilerParams(
            dimension_semantics=("parallel", "parallel", "arbitrary"),
            vmem_limit_bytes=VMEM_LIMIT),
        name=f"dilated_d{dil}",
    )(bias, bias, qkv, qkv, qkv, qkv, qkv)


def _band_bias():
    qi = jnp.arange(BAND)[:, None]
    kc = jnp.arange(2 * BAND)[None, :]
    ok = (kc >= qi) & (kc <= qi + BAND)
    first = ok & (kc >= BAND)
    both = jnp.stack([first, ok]).astype(_F32)
    both = jnp.concatenate([both, both], axis=1)
    return (1.0 - both) * NEG


def _diff_kernel(lam_ref, sw_ref, q_ref, k_ref, vt_ref, o_ref, sa_ref, sb_ref, m_ref, acc_ref,
                 *, tq, tk, lam_init):
    qi = pl.program_id(2)
    lane = lax.broadcasted_iota(jnp.int32, (tq, LANES), 1)
    low = lane < HEAD_DIM
    q = q_ref[...]
    zero = jnp.zeros_like(q)
    qq = jnp.concatenate([jnp.where(low, q, zero), jnp.where(low, zero, q)], axis=0)
    ones = jnp.ones((ONES_ROWS, tk), _BF16)
    sub_blocks = tq // tk

    def chunks(lo):
        n = tq - lo
        for c in range(0, 2 * n, DIFF_CHUNK):
            yield c, (lo + c if c < n else tq + lo + c - n), c % n

    def scores(j, buf, lo=0):
        start = j * tk if isinstance(j, int) else pl.multiple_of(j * tk, tk)
        kb = k_ref[pl.ds(start, tk), :]

        def one(c, sc):
            buf[:, c:c + DIFF_CHUNK] = lax.dot_general(
                kb, qq[sc:sc + DIFF_CHUNK], (((1,), (1,)), ((), ())), preferred_element_type=_F32)
        return [functools.partial(one, c, sc) for c, sc, _ in chunks(lo)]

    def update(j, buf, diag):
        lo = 0 if diag is None else diag * tk
        va = jnp.concatenate([vt_ref[j], ones], axis=0)

        def one(c, sc, q0):
            cols = slice(sc, sc + DIFF_CHUNK)
            s = buf[:, c:c + DIFF_CHUNK]
            if diag is not None and q0 < tk - 1:
                key = lax.broadcasted_iota(jnp.int32, s.shape, 0)
                qry = lax.broadcasted_iota(jnp.int32, s.shape, 1) + q0
                s = jnp.where(key <= qry, s, NEG)
            m = m_ref[:, cols]
            m_new = jnp.maximum(m, jnp.max(s, axis=0, keepdims=True))
            alpha = jnp.exp(m - m_new)
            p = jnp.exp(s - m_new).astype(_BF16)
            acc_ref[:, cols] = alpha * acc_ref[:, cols] + jnp.dot(va, p,
                                                                  preferred_element_type=_F32)
            m_ref[:, cols] = m_new
        return [functools.partial(one, c, sc, q0) for c, sc, q0 in chunks(lo)]

    def interleave(first, second):
        for i in range(max(len(first), len(second))):
            for stream in (first, second):
                if i < len(stream):
                    stream[i]()

    m_ref[...] = jnp.full(m_ref.shape, NEG, _F32)
    acc_ref[...] = jnp.zeros(acc_ref.shape, _F32)
    interleave(scores(0, sa_ref), [])

    def body(t, carry):
        j = sub_blocks * t
        for d in range(sub_blocks):
            nxt, cur = (sb_ref, sa_ref) if d % 2 == 0 else (sa_ref, sb_ref)
            interleave(scores(j + d + 1, nxt), update(j + d, cur, None))
        return carry

    lax.fori_loop(0, qi, body, 0)
    for d in range(sub_blocks):
        nxt, cur = (sb_ref, sa_ref) if d % 2 == 0 else (sa_ref, sb_ref)
        nxt_scores = (scores(sub_blocks * qi + d + 1, nxt, (d + 1) * tk)
                      if d + 1 < sub_blocks else [])
        interleave(nxt_scores, update(sub_blocks * qi + d, cur, d))

    acc = acc_ref[...]
    lp = lam_ref[...]
    lam = (jnp.exp(jnp.sum(lp[0:1] * lp[1:2], axis=1, keepdims=True))
           - jnp.exp(jnp.sum(lp[2:3] * lp[3:4], axis=1, keepdims=True)) + lam_init)
    num, den = acc[:LANES], acc[LANES:LANES + 1]
    ot = num[:, :tq] / den[:, :tq] - lam * (num[:, tq:] / den[:, tq:])
    o = ot.T
    o_ref[...] = _rms(o, sw_ref[...], SUBLN_EPS) * (1.0 - lam_init)


def _diff_call(qk, vt, lam_p, subln_w, layer, batch, seq, tq=DIFF_TQ, tk=DIFF_TK):
    assert tq % tk == 0 and (tq // tk) % 2 == 0
    m = batch * seq
    nq = seq // tq
    nk = seq // tk
    kern = functools.partial(_diff_kernel, tq=tq, tk=tk, lam_init=_lambda_init(layer))
    return pl.pallas_call(
        kern,
        out_shape=jax.ShapeDtypeStruct((m, D_MODEL), _F32),
        grid=(batch, B_HEADS, nq),
        in_specs=[
            pl.BlockSpec((None, 4, HEAD_DIM), lambda b, h, i: (layer, 0, 0)),
            pl.BlockSpec((None, 1, LANES), lambda b, h, i: (layer, 0, 0)),
            pl.BlockSpec((None, tq, LANES), lambda b, h, i: (0, b * nq + i, h)),
            pl.BlockSpec((None, seq, LANES), lambda b, h, i: (1, b, h)),
            pl.BlockSpec((nk, LANES, tk), lambda b, h, i: (b, h, 0)),
        ],
        out_specs=pl.BlockSpec((tq, LANES), lambda b, h, i: (b * nq + i, h)),
        scratch_shapes=[pltpu.VMEM((tk, 2 * tq), _F32), pltpu.VMEM((tk, 2 * tq), _F32),
                        pltpu.VMEM((1, 2 * tq), _F32),
                        pltpu.VMEM((LANES + ONES_ROWS, 2 * tq), _F32)],
        compiler_params=pltpu.CompilerParams(
            dimension_semantics=("parallel", "parallel", "arbitrary"),
            vmem_limit_bytes=VMEM_LIMIT),
        name="diff_attn",
    )(lam_p, subln_w, qk, qk, vt)


def _expand_heads(w, e):
    hi = w.astype(_BF16)
    lo = (w - hi.astype(_F32)).astype(_BF16)
    return (jnp.dot(hi, e, preferred_element_type=_F32)
            + jnp.dot(lo, e, preferred_element_type=_F32))


def _merge_kernel(x_ref, o1_ref, o2_ref, o3_ref, s1_ref, s2_ref, s3_ref, ob_ref, nw_ref,
                  wza_ref, wzb_ref, wga_ref, wgb_ref, wpa_ref, wpb_ref, wo_ref, e_ref, fw_ref,
                  out_ref, on2_ref, on3_ref, sn2_ref, sn3_ref, *, final):
    x = x_ref[...]
    tm = x.shape[0]
    hdn = _rms(x, nw_ref[...], RMS_EPS).astype(_BF16)

    n_blk = D_MODEL // LANES
    for d, src, dst in ((DILATIONS[1], o2_ref, on2_ref), (DILATIONS[2], o3_ref, on3_ref)):
        for r in range(d):
            for blk in range(n_blk):
                dst[blk, pl.ds(r, tm // d, stride=d), :] = src[r, :, blk * LANES:(blk + 1) * LANES]
    for d, src, dst in ((DILATIONS[1], s2_ref, sn2_ref), (DILATIONS[2], s3_ref, sn3_ref)):
        for r in range(d):
            dst[pl.ds(r, tm // d, stride=d), :] = src[r]
    o1 = o1_ref[...]
    outs = ([o1[:, blk * LANES:(blk + 1) * LANES] for blk in range(n_blk)],
            [on2_ref[blk] for blk in range(n_blk)], [on3_ref[blk] for blk in range(n_blk)])

    sts = (s1_ref[...], sn2_ref[...], sn3_ref[...])
    lane = lax.broadcasted_iota(jnp.int32, sts[0].shape, 1)
    mx = jnp.maximum(jnp.maximum(sts[0], sts[1]), sts[2])
    wts = [jnp.exp(s - mx) for s in sts]
    dens = [pltpu.roll(s, LANES - A_HEADS, axis=1) * w for s, w in zip(sts, wts)]
    inv = 1.0 / (dens[0] + dens[1] + dens[2])
    e = e_ref[...]
    ya_blocks = None
    for w, o_blocks in zip(wts, outs):
        wn = _expand_heads(jnp.where(lane < A_HEADS, w * inv, 0.0), e)
        terms = [wn[:, blk * LANES:(blk + 1) * LANES] * o_blocks[blk] for blk in range(n_blk)]
        ya_blocks = terms if ya_blocks is None else [a + t for a, t in zip(ya_blocks, terms)]
    ya = jnp.concatenate(ya_blocks, axis=1)

    za = jnp.dot(hdn, wza_ref[...], preferred_element_type=_F32)
    ya = (ya * (za * _sigmoid(za))).astype(_BF16)
    pa = jnp.dot(ya, wpa_ref[...], preferred_element_type=_F32)
    ga = jnp.dot(hdn, wga_ref[...], preferred_element_type=_F32)
    merged = _sigmoid(ga) * pa

    zb = jnp.dot(hdn, wzb_ref[...], preferred_element_type=_F32)
    yb = (ob_ref[...] * (zb * _sigmoid(zb))).astype(_BF16)
    pb = jnp.dot(yb, wpb_ref[...], preferred_element_type=_F32)
    gb = jnp.dot(hdn, wgb_ref[...], preferred_element_type=_F32)
    merged = (merged + _sigmoid(gb) * pb).astype(_BF16)

    y = x + jnp.dot(merged, wo_ref[...], preferred_element_type=_F32)
    if final:
        y = _rms(y, fw_ref[...], RMS_EPS)
    out_ref[...] = y


def _merge_call(x2, os_, sts, ob, norm_w, w_in_bf, wpa_bf, wpb_bf, wo_bf, expand, final_w,
                layer, final, seq, tm=256):
    m = x2.shape[0]
    tiles = seq // tm
    row = lambda width: pl.BlockSpec((tm, width), lambda i: (i, 0))
    pat = lambda d, width: pl.BlockSpec((None, d, tm // d, width),
                                        lambda i: (i // tiles, 0, i % tiles, 0))
    d1, d4, d16 = DILATIONS
    o1 = os_[0].reshape(m, D_MODEL)
    s1 = sts[0].reshape(m, LANES)
    once = pl.Buffered(1)
    w_in_col = lambda c: pl.BlockSpec((None, D_MODEL, D_MODEL), lambda i: (layer, 0, c),
                                      pipeline_mode=once)
    w_sq = pl.BlockSpec((None, D_MODEL, D_MODEL), lambda i: (layer, 0, 0), pipeline_mode=once)
    return pl.pallas_call(
        functools.partial(_merge_kernel, final=final),
        out_shape=jax.ShapeDtypeStruct((m, D_MODEL), _F32),
        grid=(m // tm,),
        in_specs=[
            row(D_MODEL), row(D_MODEL), pat(d4, D_MODEL), pat(d16, D_MODEL),
            row(LANES), pat(d4, LANES), pat(d16, LANES), row(D_MODEL),
            pl.BlockSpec((None, 1, D_MODEL), lambda i: (layer, 0, 0)),
            w_in_col(3), w_in_col(7), w_in_col(8), w_in_col(9),
            w_sq, w_sq, w_sq,
            pl.BlockSpec((LANES, D_MODEL), lambda i: (0, 0), pipeline_mode=once),
            pl.BlockSpec((1, D_MODEL), lambda i: (0, 0)),
        ],
        out_specs=row(D_MODEL),
        scratch_shapes=[pltpu.VMEM((D_MODEL // LANES, tm, LANES), _F32)] * 2
                       + [pltpu.VMEM((tm, LANES), _F32)] * 2,
        compiler_params=pltpu.CompilerParams(
            dimension_semantics=("parallel",), vmem_limit_bytes=VMEM_LIMIT),
        name="merge_out",
    )(x2, o1, os_[1], os_[2], s1, sts[1], sts[2], ob, norm_w, w_in_bf, w_in_bf, w_in_bf, w_in_bf,
      wpa_bf, wpb_bf, wo_bf, expand, final_w)


def _rope_tables(seq):
    half = ROPE_DIM // 2
    inv = 1.0 / (ROPE_THETA ** (jnp.arange(0, ROPE_DIM, 2, dtype=_F32) / ROPE_DIM))
    ang = jnp.arange(seq, dtype=_F32)[:, None] * inv[None, :]
    cos, sin = jnp.cos(ang), jnp.sin(ang)
    rest = HEAD_DIM - ROPE_DIM
    z = lambda n: jnp.zeros((seq, n), _F32)
    cos_h = jnp.concatenate([cos, cos, jnp.ones((seq, rest), _F32)], axis=1)
    sa_h = jnp.concatenate([z(half), sin, z(rest)], axis=1)
    sb_h = jnp.concatenate([-sin, z(half), z(rest)], axis=1)
    rep = LANES // HEAD_DIM
    return tuple(jnp.tile(t, (1, rep)) for t in (cos_h, sa_h, sb_h))


def kernel(x, norm_w, w_in, lambda_q1, lambda_k1, lambda_q2, lambda_k2, subln_w,
           w_proj_a, w_proj_b, w_out, final_norm_w):
    batch, seq, d = x.shape
    depth = norm_w.shape[0]
    assert d == D_MODEL and seq % (BAND * DILATIONS[-1]) == 0
    assert w_in.shape[-1] == N_COL_BLOCKS * D_MODEL

    w_in_bf = w_in.astype(_BF16)
    wpa_bf = w_proj_a.astype(_BF16)
    wpb_bf = w_proj_b.astype(_BF16)
    wo_bf = w_out.astype(_BF16)
    lam_p = jnp.stack([lambda_q1, lambda_k1, lambda_q2, lambda_k2], axis=1)
    tabs = _rope_tables(seq)
    bias = _band_bias()
    head_of_col = jnp.arange(D_MODEL)[None, :] // HEAD_DIM
    expand = (jnp.arange(LANES)[:, None] == head_of_col).astype(_BF16)
    final_w = final_norm_w.reshape(1, D_MODEL)
    norm_w = norm_w.reshape(depth, 1, D_MODEL)
    subln_w = subln_w.reshape(depth, 1, LANES)

    h = x.reshape(batch * seq, D_MODEL)
    for layer in range(depth):
        *a_layouts, qk, vt = _proj_call(h, norm_w, w_in_bf, layer, tabs, batch, seq)
        os_, sts = zip(*[_dilated_call(a, bias, dil, batch, seq)
                         for a, dil in zip(a_layouts, DILATIONS)])
        ob = _diff_call(qk, vt, lam_p, subln_w, layer, batch, seq)
        h = _merge_call(h, os_, sts, ob, norm_w, w_in_bf, wpa_bf, wpb_bf, wo_bf, expand,
                        final_w, layer, layer == depth - 1, seq)
    return h.reshape(batch, seq, D_MODEL)
```

```python
import functools
import math

import jax
import jax.numpy as jnp
from jax import lax
from jax.experimental import pallas as pl
from jax.experimental.pallas import tpu as pltpu

D_MODEL = 1024
HEAD_DIM = 64
A_HEADS = 16
B_HEADS = 8
DILATIONS = (1, 4, 16)
BAND = 128
ROPE_THETA = 500000.0
ROPE_DIM = HEAD_DIM // 4
RMS_EPS = 1e-6
SUBLN_EPS = 1e-5
NEG = -1e30
LANES = 128
N_COL_BLOCKS = 10
VMEM_LIMIT = 56 * 1024 * 1024
DIFF_TQ = 2048
DIL_QB = 4
DIFF_TK = 512
DIFF_CHUNK = 256
DIFF_LAG = 6
ONES_ROWS = 16

_F32 = jnp.float32
_BF16 = jnp.bfloat16


def _lambda_init(layer):
    return 0.8 - 0.6 * math.exp(-0.3 * layer)


def _rms(x, w, eps):
    return x * lax.rsqrt(jnp.mean(x * x, axis=-1, keepdims=True) + eps) * w


def _sigmoid(z):
    return 1.0 / (1.0 + jnp.exp(-z))


def _proj_kernel(x_ref, nw_ref, wqa_ref, wka_ref, wva_ref, wqb_ref, wkb_ref, wvb_ref,
                 cos_ref, sa_ref, sb_ref, a1_ref, a4_ref, a16_ref, qk_ref, vt_ref,
                 scr0_ref, scr1_ref):
    tm = x_ref.shape[0]
    hdn = _rms(x_ref[...], nw_ref[...], RMS_EPS).astype(_BF16)
    cos, sa, sb = cos_ref[...], sa_ref[...], sb_ref[...]
    q_scale = HEAD_DIM ** -0.5
    tabs = {True: (cos * q_scale, sa * q_scale, sb * q_scale), False: (cos, sa, sb)}

    def lane_blocks(u):
        return [u[:, blk * LANES:(blk + 1) * LANES] for blk in range(D_MODEL // LANES)]

    def rope(u, is_q):
        c, s_a, s_b = tabs[is_q]
        return [t * c + pltpu.roll(t, ROPE_DIM // 2, axis=1) * s_a
                + pltpu.roll(t, LANES - ROPE_DIM // 2, axis=1) * s_b for t in lane_blocks(u)]

    d4, d16 = DILATIONS[1], DILATIONS[2]
    step = d16 // d4

    def residue_layouts(blocks, which):
        for blk, t in enumerate(blocks):
            sl = slice(blk * LANES, (blk + 1) * LANES)
            a1_ref[which, :, sl] = t.astype(_BF16)
            scr0_ref[blk] = t
            for c in range(d4):
                x = scr0_ref[blk, pl.ds(c, tm // d4, stride=d4), :]
                a4_ref[which, c, :, sl] = x.astype(_BF16)
                scr1_ref[blk, c * (tm // d4):(c + 1) * (tm // d4), :] = x
            for r in range(d16):
                c, b = r % d4, r // d4
                x = scr1_ref[blk, pl.ds(c * (tm // d4) + b, tm // d16, stride=step), :]
                a16_ref[which, r, :, sl] = x.astype(_BF16)

    def natural(ref, blocks, which):
        for blk, t in enumerate(blocks):
            ref[which, :, blk * LANES:(blk + 1) * LANES] = t.astype(_BF16)

    dot = lambda w_ref: jnp.dot(hdn, w_ref[...], preferred_element_type=_F32)
    residue_layouts(rope(dot(wqa_ref), True), 0)
    residue_layouts(rope(dot(wka_ref), False), 1)
    residue_layouts(lane_blocks(dot(wva_ref)), 2)
    natural(qk_ref, rope(dot(wqb_ref), True), 0)
    natural(qk_ref, rope(dot(wkb_ref), False), 1)
    vt_ref[0] = dot(wvb_ref).T.astype(_BF16)


def _proj_call(x2, norm_w, w_in_bf, layer, tabs, batch, seq, tm=DIFF_TK):
    m = x2.shape[0]
    tiles = seq // tm
    cos_t, sa_t, sb_t = tabs
    once = pl.Buffered(1)
    w_col = lambda c: pl.BlockSpec((None, D_MODEL, D_MODEL), lambda i: (layer, 0, c),
                                   pipeline_mode=once)
    tab_spec = pl.BlockSpec((tm, LANES), lambda i: (i % tiles, 0))
    a_shape = lambda d: jax.ShapeDtypeStruct((3, batch, d, seq // d, D_MODEL), _BF16)
    a_spec = lambda d: pl.BlockSpec((3, None, d, tm // d, D_MODEL),
                                    lambda i: (0, i // tiles, 0, i % tiles, 0))
    d1, d4, d16 = DILATIONS
    return pl.pallas_call(
        _proj_kernel,
        out_shape=(a_shape(d1), a_shape(d4), a_shape(d16),
                   jax.ShapeDtypeStruct((2, m, D_MODEL), _BF16),
                   jax.ShapeDtypeStruct((m // tm, D_MODEL, tm), _BF16)),
        grid=(m // tm,),
        in_specs=[
            pl.BlockSpec((tm, D_MODEL), lambda i: (i, 0)),
            pl.BlockSpec((None, 1, D_MODEL), lambda i: (layer, 0, 0)),
            w_col(0), w_col(1), w_col(2), w_col(4), w_col(5), w_col(6),
            tab_spec, tab_spec, tab_spec,
        ],
        out_specs=(pl.BlockSpec((3, None, None, tm, D_MODEL),
                                lambda i: (0, i // tiles, 0, i % tiles, 0)),
                   a_spec(d4), a_spec(d16),
                   pl.BlockSpec((2, tm, D_MODEL), lambda i: (0, i, 0)),
                   pl.BlockSpec((1, D_MODEL, tm), lambda i: (i, 0, 0))),
        scratch_shapes=[pltpu.VMEM((D_MODEL // LANES, tm, LANES), _F32)] * 2,
        compiler_params=pltpu.CompilerParams(
            dimension_semantics=("parallel",), vmem_limit_bytes=VMEM_LIMIT),
        name="qkv_proj",
    )(x2, norm_w, w_in_bf, w_in_bf, w_in_bf, w_in_bf, w_in_bf, w_in_bf, cos_t, sa_t, sb_t)


def _dilated_kernel(bias0_ref, bias_ref, q_ref, kp_ref, kc_ref, vp_ref, vc_ref, o_ref, st_ref):
    lane = lax.broadcasted_iota(jnp.int32, (BAND, LANES), 1)
    low = lane < HEAD_DIM
    ones = jnp.ones((2 * BAND, LANES), _BF16)

    def band_scores(u, hp):
        rows = slice(u * BAND, (u + 1) * BAND)
        sl = slice(hp * LANES, (hp + 1) * LANES)
        q = q_ref[rows, sl]
        zero = jnp.zeros_like(q)
        qq = jnp.concatenate([jnp.where(low, q, zero), jnp.where(low, zero, q)], axis=0)
        k_prev = kp_ref[:, sl] if u == 0 else kc_ref[(u - 1) * BAND:u * BAND, sl]
        kk = jnp.concatenate([k_prev, kc_ref[rows, sl]], axis=0)
        bias = (bias0_ref if u == 0 else bias_ref)[...]
        return lax.dot_general(qq, kk, (((1,), (1,)), ((), ())),
                               preferred_element_type=_F32) + bias

    work = [(u, hp) for u in range(DIL_QB) for hp in range(A_HEADS // 2)]
    s = band_scores(*work[0])
    st = None
    for i, (u, hp) in enumerate(work):
        s_next = band_scores(*work[i + 1]) if i + 1 < len(work) else None
        rows = slice(u * BAND, (u + 1) * BAND)
        sl = slice(hp * LANES, (hp + 1) * LANES)
        if hp == 0:
            st = jnp.zeros((BAND, LANES), _F32)
        v_prev = vp_ref[:, sl] if u == 0 else vc_ref[(u - 1) * BAND:u * BAND, sl]
        va = jnp.concatenate([jnp.concatenate([v_prev, vc_ref[rows, sl]], axis=0), ones],
                             axis=1)
        m = jnp.max(s, axis=1, keepdims=True)
        p = jnp.exp(s - m).astype(_BF16)
        pv = jnp.dot(p, va, preferred_element_type=_F32)
        num, den = pv[:, :LANES], pv[:, LANES:]
        o_ref[rows, sl] = jnp.where(low, num[:BAND], num[BAND:])
        for e in range(2):
            h = 2 * hp + e
            st = jnp.where(lane == h, m[e * BAND:(e + 1) * BAND], st)
            st = jnp.where(lane == A_HEADS + h, den[e * BAND:(e + 1) * BAND], st)
        if hp == A_HEADS // 2 - 1:
            st_ref[rows, :] = st
        s = s_next


def _dilated_call(qkv, bias, dil, batch, seq):
    per_class = seq // dil
    rows = DIL_QB * BAND
    steps = per_class // rows
    cur = lambda which: pl.BlockSpec((None, None, None, rows, D_MODEL),
                                     lambda b, r, n: (which, b, r, n, 0))
    prev = lambda which: pl.BlockSpec(
        (None, None, None, BAND, D_MODEL),
        lambda b, r, n: (which, b, r, jnp.maximum(n * DIL_QB - 1, 0), 0))
    return pl.pallas_call(
        _dilated_kernel,
        out_shape=(jax.ShapeDtypeStruct((batch, dil, per_class, D_MODEL), _F32),
                   jax.ShapeDtypeStruct((batch, dil, per_class, LANES), _F32)),
        grid=(batch, dil, steps),
        in_specs=[
            pl.BlockSpec((None, 2 * BAND, 2 * BAND), lambda b, r, n: (jnp.minimum(n, 1), 0, 0)),
            pl.BlockSpec((None, 2 * BAND, 2 * BAND), lambda b, r, n: (1, 0, 0)),
            cur(0), prev(1), cur(1), prev(2), cur(2),
        ],
        out_specs=(pl.BlockSpec((None, None, rows, D_MODEL), lambda b, r, n: (b, r, n, 0)),
                   pl.BlockSpec((None, None, rows, LANES), lambda b, r, n: (b, r, n, 0))),
        compiler_params=pltpu.CompilerParams(
            dimension_semantics=("parallel", "parallel", "arbitrary"),
            vmem_limit_bytes=VMEM_LIMIT),
        name=f"dilated_d{dil}",
    )(bias, bias, qkv, qkv, qkv, qkv, qkv)


def _band_bias():
    qi = jnp.arange(BAND)[:, None]
    kc = jnp.arange(2 * BAND)[None, :]
    ok = (kc >= qi) & (kc <= qi + BAND)
    first = ok & (kc >= BAND)
    both = jnp.stack([first, ok]).astype(_F32)
    both = jnp.concatenate([both, both], axis=1)
    return (1.0 - both) * NEG


def _diff_kernel(lam_ref, sw_ref, q_ref, k_ref, vt_ref, o_ref, m_ref, acc_ref, *s_refs,
                 tq, tk, lam_init):
    qi = pl.program_id(2)
    lane = lax.broadcasted_iota(jnp.int32, (tq, LANES), 1)
    low = lane < HEAD_DIM
    q = q_ref[...]
    zero = jnp.zeros_like(q)
    qq = jnp.concatenate([jnp.where(low, q, zero), jnp.where(low, zero, q)], axis=0)
    ones = jnp.ones((ONES_ROWS, tk), _BF16)
    sub_blocks = tq // tk

    def chunks(lo):
        n = tq - lo
        return [(c, (lo + c if c < n else tq + lo + c - n), c % n)
                for c in range(0, 2 * n, DIFF_CHUNK)]

    def score_chunk(j, lo, idx):
        c, sc, _ = chunks(lo)[idx]
        start = j * tk if isinstance(j, int) else pl.multiple_of(j * tk, tk)
        s_refs[c // DIFF_CHUNK][...] = lax.dot_general(
            k_ref[pl.ds(start, tk), :], qq[sc:sc + DIFF_CHUNK], (((1,), (1,)), ((), ())),
            preferred_element_type=_F32)

    def update_chunk(j, diag, idx):
        lo = 0 if diag is None else diag * tk
        c, sc, q0 = chunks(lo)[idx]
        cols = slice(sc, sc + DIFF_CHUNK)
        s = s_refs[c // DIFF_CHUNK][...]
        if diag is not None and q0 < tk - 1:
            key = lax.broadcasted_iota(jnp.int32, s.shape, 0)
            qry = lax.broadcasted_iota(jnp.int32, s.shape, 1) + q0
            s = jnp.where(key <= qry, s, NEG)
        m = m_ref[:, cols]
        m_new = jnp.maximum(m, jnp.max(s, axis=0, keepdims=True))
        alpha = jnp.exp(m - m_new)
        p = jnp.exp(s - m_new).astype(_BF16)
        va = jnp.concatenate([vt_ref[j], ones], axis=0)
        acc_ref[:, cols] = alpha * acc_ref[:, cols] + jnp.dot(va, p, preferred_element_type=_F32)
        m_ref[:, cols] = m_new

    n_full = len(chunks(0))
    m_ref[...] = jnp.full(m_ref.shape, NEG, _F32)
    acc_ref[...] = jnp.zeros(acc_ref.shape, _F32)
    for idx in range(DIFF_LAG):
        score_chunk(0, 0, idx)

    def body(t, carry):
        for u in range(sub_blocks):
            j = sub_blocks * t + u
            for idx in range(n_full):
                ahead = idx + DIFF_LAG
                if ahead < n_full:
                    score_chunk(j, 0, ahead)
                else:
                    score_chunk(j + 1, 0, ahead - n_full)
                update_chunk(j, None, idx)
        return carry

    j0 = sub_blocks * qi
    lax.fori_loop(0, qi, body, 0)
    flat = [(d, idx) for d in range(sub_blocks) for idx in range(len(chunks(d * tk)))]
    for pos, (d, idx) in enumerate(flat):
        if pos + DIFF_LAG < len(flat):
            d_a, idx_a = flat[pos + DIFF_LAG]
            score_chunk(j0 + d_a, d_a * tk, idx_a)
        update_chunk(j0 + d, d, idx)


    acc = acc_ref[...]
    lp = lam_ref[...]
    lam = (jnp.exp(jnp.sum(lp[0:1] * lp[1:2], axis=1, keepdims=True))
           - jnp.exp(jnp.sum(lp[2:3] * lp[3:4], axis=1, keepdims=True)) + lam_init)
    num, den = acc[:LANES], acc[LANES:LANES + 1]
    ot = num[:, :tq] / den[:, :tq] - lam * (num[:, tq:] / den[:, tq:])
    o = ot.T
    o_ref[...] = _rms(o, sw_ref[...], SUBLN_EPS) * (1.0 - lam_init)


def _diff_call(qk, vt, lam_p, subln_w, layer, batch, seq, tq=DIFF_TQ, tk=DIFF_TK):
    assert tq % tk == 0 and (2 * tk) % DIFF_CHUNK == 0
    m = batch * seq
    nq = seq // tq
    nk = seq // tk
    kern = functools.partial(_diff_kernel, tq=tq, tk=tk, lam_init=_lambda_init(layer))
    return pl.pallas_call(
        kern,
        out_shape=jax.ShapeDtypeStruct((m, D_MODEL), _F32),
        grid=(batch, B_HEADS, nq),
        in_specs=[
            pl.BlockSpec((None, 4, HEAD_DIM), lambda b, h, i: (layer, 0, 0)),
            pl.BlockSpec((None, 1, LANES), lambda b, h, i: (layer, 0, 0)),
            pl.BlockSpec((None, tq, LANES), lambda b, h, i: (0, b * nq + i, h)),
            pl.BlockSpec((None, seq, LANES), lambda b, h, i: (1, b, h)),
            pl.BlockSpec((nk, LANES, tk), lambda b, h, i: (b, h, 0)),
        ],
        out_specs=pl.BlockSpec((tq, LANES), lambda b, h, i: (b * nq + i, h)),
        scratch_shapes=[pltpu.VMEM((1, 2 * tq), _F32),
                        pltpu.VMEM((LANES + ONES_ROWS, 2 * tq), _F32)]
                       + [pltpu.VMEM((tk, DIFF_CHUNK), _F32)] * (2 * tq // DIFF_CHUNK),
        compiler_params=pltpu.CompilerParams(
            dimension_semantics=("parallel", "parallel", "arbitrary"),
            vmem_limit_bytes=VMEM_LIMIT),
        name="diff_attn",
    )(lam_p, subln_w, qk, qk, vt)


def _expand_heads(w, e):
    hi = w.astype(_BF16)
    lo = (w - hi.astype(_F32)).astype(_BF16)
    return jnp.dot(jnp.concatenate([hi, lo], axis=1), e, preferred_element_type=_F32)


def _merge_kernel(x_ref, o1_ref, o2_ref, o3_ref, s1_ref, s2_ref, s3_ref, ob_ref, nw_ref,
                  wza_ref, wzb_ref, wga_ref, wgb_ref, wpa_ref, wpb_ref, wo_ref, e_ref, fw_ref,
                  out_ref, on2_ref, on3_ref, sn2_ref, sn3_ref, *, final):
    x = x_ref[...]
    tm = x.shape[0]
    hdn = _rms(x, nw_ref[...], RMS_EPS).astype(_BF16)

    n_blk = D_MODEL // LANES
    for d, src, dst in ((DILATIONS[1], o2_ref, on2_ref), (DILATIONS[2], o3_ref, on3_ref)):
        for r in range(d):
            for blk in range(n_blk):
                dst[blk, pl.ds(r, tm // d, stride=d), :] = src[r, :, blk * LANES:(blk + 1) * LANES]
    for d, src, dst in ((DILATIONS[1], s2_ref, sn2_ref), (DILATIONS[2], s3_ref, sn3_ref)):
        for r in range(d):
            dst[pl.ds(r, tm // d, stride=d), :] = src[r]
    o1 = o1_ref[...]
    outs = ([o1[:, blk * LANES:(blk + 1) * LANES] for blk in range(n_blk)],
            [on2_ref[blk] for blk in range(n_blk)], [on3_ref[blk] for blk in range(n_blk)])

    sts = (s1_ref[...], sn2_ref[...], sn3_ref[...])
    lane = lax.broadcasted_iota(jnp.int32, sts[0].shape, 1)
    mx = jnp.maximum(jnp.maximum(sts[0], sts[1]), sts[2])
    wts = [jnp.exp(s - mx) for s in sts]
    dens = [pltpu.roll(s, LANES - A_HEADS, axis=1) * w for s, w in zip(sts, wts)]
    inv = 1.0 / (dens[0] + dens[1] + dens[2])
    e = e_ref[...]
    ya_blocks = None
    for w, o_blocks in zip(wts, outs):
        wn = _expand_heads(jnp.where(lane < A_HEADS, w * inv, 0.0), e)
        terms = [wn[:, blk * LANES:(blk + 1) * LANES] * o_blocks[blk] for blk in range(n_blk)]
        ya_blocks = terms if ya_blocks is None else [a + t for a, t in zip(ya_blocks, terms)]
    ya = jnp.concatenate(ya_blocks, axis=1)

    za = jnp.dot(hdn, wza_ref[...], preferred_element_type=_F32)
    ya = (ya * (za * _sigmoid(za))).astype(_BF16)
    pa = jnp.dot(ya, wpa_ref[...], preferred_element_type=_F32)
    ga = jnp.dot(hdn, wga_ref[...], preferred_element_type=_F32)
    merged = _sigmoid(ga) * pa

    zb = jnp.dot(hdn, wzb_ref[...], preferred_element_type=_F32)
    yb = (ob_ref[...] * (zb * _sigmoid(zb))).astype(_BF16)
    pb = jnp.dot(yb, wpb_ref[...], preferred_element_type=_F32)
    gb = jnp.dot(hdn, wgb_ref[...], preferred_element_type=_F32)
    merged = (merged + _sigmoid(gb) * pb).astype(_BF16)

    y = x + jnp.dot(merged, wo_ref[...], preferred_element_type=_F32)
    if final:
        y = _rms(y, fw_ref[...], RMS_EPS)
    out_ref[...] = y


def _merge_call(x2, os_, sts, ob, norm_w, w_in_bf, wpa_bf, wpb_bf, wo_bf, expand, final_w,
                layer, final, seq, tm=256):
    m = x2.shape[0]
    tiles = seq // tm
    row = lambda width: pl.BlockSpec((tm, width), lambda i: (i, 0))
    pat = lambda d, width: pl.BlockSpec((None, d, tm // d, width),
                                        lambda i: (i // tiles, 0, i % tiles, 0))
    d1, d4, d16 = DILATIONS
    o1 = os_[0].reshape(m, D_MODEL)
    s1 = sts[0].reshape(m, LANES)
    once = pl.Buffered(1)
    w_in_col = lambda c: pl.BlockSpec((None, D_MODEL, D_MODEL), lambda i: (layer, 0, c),
                                      pipeline_mode=once)
    w_sq = pl.BlockSpec((None, D_MODEL, D_MODEL), lambda i: (layer, 0, 0), pipeline_mode=once)
    return pl.pallas_call(
        functools.partial(_merge_kernel, final=final),
        out_shape=jax.ShapeDtypeStruct((m, D_MODEL), _F32),
        grid=(m // tm,),
        in_specs=[
            row(D_MODEL), row(D_MODEL), pat(d4, D_MODEL), pat(d16, D_MODEL),
            row(LANES), pat(d4, LANES), pat(d16, LANES), row(D_MODEL),
            pl.BlockSpec((None, 1, D_MODEL), lambda i: (layer, 0, 0)),
            w_in_col(3), w_in_col(7), w_in_col(8), w_in_col(9),
            w_sq, w_sq, w_sq,
            pl.BlockSpec((2 * LANES, D_MODEL), lambda i: (0, 0), pipeline_mode=once),
            pl.BlockSpec((1, D_MODEL), lambda i: (0, 0)),
        ],
        out_specs=row(D_MODEL),
        scratch_shapes=[pltpu.VMEM((D_MODEL // LANES, tm, LANES), _F32)] * 2
                       + [pltpu.VMEM((tm, LANES), _F32)] * 2,
        compiler_params=pltpu.CompilerParams(
            dimension_semantics=("parallel",), vmem_limit_bytes=VMEM_LIMIT),
        name="merge_out",
    )(x2, o1, os_[1], os_[2], s1, sts[1], sts[2], ob, norm_w, w_in_bf, w_in_bf, w_in_bf, w_in_bf,
      wpa_bf, wpb_bf, wo_bf, expand, final_w)


def _rope_tables(seq):
    half = ROPE_DIM // 2
    inv = 1.0 / (ROPE_THETA ** (jnp.arange(0, ROPE_DIM, 2, dtype=_F32) / ROPE_DIM))
    ang = jnp.arange(seq, dtype=_F32)[:, None] * inv[None, :]
    cos, sin = jnp.cos(ang), jnp.sin(ang)
    rest = HEAD_DIM - ROPE_DIM
    z = lambda n: jnp.zeros((seq, n), _F32)
    cos_h = jnp.concatenate([cos, cos, jnp.ones((seq, rest), _F32)], axis=1)
    sa_h = jnp.concatenate([z(half), sin, z(rest)], axis=1)
    sb_h = jnp.concatenate([-sin, z(half), z(rest)], axis=1)
    rep = LANES // HEAD_DIM
    return tuple(jnp.tile(t, (1, rep)) for t in (cos_h, sa_h, sb_h))


def kernel(x, norm_w, w_in, lambda_q1, lambda_k1, lambda_q2, lambda_k2, subln_w,
           w_proj_a, w_proj_b, w_out, final_norm_w):
    batch, seq, d = x.shape
    depth = norm_w.shape[0]
    assert d == D_MODEL and seq % (BAND * DILATIONS[-1]) == 0
    assert w_in.shape[-1] == N_COL_BLOCKS * D_MODEL

    w_in_bf = w_in.astype(_BF16)
    wpa_bf = w_proj_a.astype(_BF16)
    wpb_bf = w_proj_b.astype(_BF16)
    wo_bf = w_out.astype(_BF16)
    lam_p = jnp.stack([lambda_q1, lambda_k1, lambda_q2, lambda_k2], axis=1)
    tabs = _rope_tables(seq)
    bias = _band_bias()
    head_of_col = jnp.arange(D_MODEL)[None, :] // HEAD_DIM
    expand = (jnp.arange(2 * LANES)[:, None] % LANES == head_of_col).astype(_BF16)
    final_w = final_norm_w.reshape(1, D_MODEL)
    norm_w = norm_w.reshape(depth, 1, D_MODEL)
    subln_w = subln_w.reshape(depth, 1, LANES)

    h = x.reshape(batch * seq, D_MODEL)
    for layer in range(depth):
        *a_layouts, qk, vt = _proj_call(h, norm_w, w_in_bf, layer, tabs, batch, seq)
        os_, sts = zip(*[_dilated_call(a, bias, dil, batch, seq)
                         for a, dil in zip(a_layouts, DILATIONS)])
        ob = _diff_call(qk, vt, lam_p, subln_w, layer, batch, seq)
        h = _merge_call(h, os_, sts, ob, norm_w, w_in_bf, wpa_bf, wpb_bf, wo_bf, expand,
                        final_w, layer, layer == depth - 1, seq)
    return h.reshape(batch, seq, D_MODEL)
```

```python
import functools
import math

import jax
import jax.numpy as jnp
from jax import lax
from jax.experimental import pallas as pl
from jax.experimental.pallas import tpu as pltpu

D_MODEL = 1024
HEAD_DIM = 64
A_HEADS = 16
B_HEADS = 8
DILATIONS = (1, 4, 16)
BAND = 128
ROPE_THETA = 500000.0
ROPE_DIM = HEAD_DIM // 4
RMS_EPS = 1e-6
SUBLN_EPS = 1e-5
NEG = -1e30
LANES = 128
N_COL_BLOCKS = 10
VMEM_LIMIT = 56 * 1024 * 1024
DIFF_TQ = 2048
DIL_QB = 8
DIFF_TK = 512
DIFF_CHUNK = 256
DIFF_LAG = 6
ONES_ROWS = 16

_F32 = jnp.float32
_BF16 = jnp.bfloat16


def _lambda_init(layer):
    return 0.8 - 0.6 * math.exp(-0.3 * layer)


def _rms(x, w, eps):
    return x * lax.rsqrt(jnp.mean(x * x, axis=-1, keepdims=True) + eps) * w


def _sigmoid(z):
    return 1.0 / (1.0 + jnp.exp(-z))


def _proj_kernel(x_ref, nw_ref, wqa_ref, wka_ref, wva_ref, wqb_ref, wkb_ref, wvb_ref,
                 cos_ref, sa_ref, sb_ref, a1_ref, a4_ref, a16_ref, qk_ref, vt_ref,
                 scr0_ref, scr1_ref):
    tm = x_ref.shape[0]
    hdn = _rms(x_ref[...], nw_ref[...], RMS_EPS).astype(_BF16)
    cos, sa, sb = cos_ref[...], sa_ref[...], sb_ref[...]
    q_scale = HEAD_DIM ** -0.5
    tabs = {True: (cos * q_scale, sa * q_scale, sb * q_scale), False: (cos, sa, sb)}

    def lane_blocks(u):
        return [u[:, blk * LANES:(blk + 1) * LANES] for blk in range(D_MODEL // LANES)]

    def rope(u, is_q):
        c, s_a, s_b = tabs[is_q]
        return [t * c + pltpu.roll(t, ROPE_DIM // 2, axis=1) * s_a
                + pltpu.roll(t, LANES - ROPE_DIM // 2, axis=1) * s_b for t in lane_blocks(u)]

    d4, d16 = DILATIONS[1], DILATIONS[2]
    step = d16 // d4

    def residue_layouts(blocks, which):
        for blk, t in enumerate(blocks):
            sl = slice(blk * LANES, (blk + 1) * LANES)
            a1_ref[which, :, sl] = t.astype(_BF16)
            scr0_ref[blk] = t
            for c in range(d4):
                x = scr0_ref[blk, pl.ds(c, tm // d4, stride=d4), :]
                a4_ref[which, c, :, sl] = x.astype(_BF16)
                scr1_ref[blk, c * (tm // d4):(c + 1) * (tm // d4), :] = x
            for r in range(d16):
                c, b = r % d4, r // d4
                x = scr1_ref[blk, pl.ds(c * (tm // d4) + b, tm // d16, stride=step), :]
                a16_ref[which, r, :, sl] = x.astype(_BF16)

    def natural(ref, blocks, which):
        for blk, t in enumerate(blocks):
            ref[which, :, blk * LANES:(blk + 1) * LANES] = t.astype(_BF16)

    dot = lambda w_ref: jnp.dot(hdn, w_ref[...], preferred_element_type=_F32)
    residue_layouts(rope(dot(wqa_ref), True), 0)
    residue_layouts(rope(dot(wka_ref), False), 1)
    residue_layouts(lane_blocks(dot(wva_ref)), 2)
    natural(qk_ref, rope(dot(wqb_ref), True), 0)
    natural(qk_ref, rope(dot(wkb_ref), False), 1)
    vt_ref[0] = dot(wvb_ref).T.astype(_BF16)


def _proj_call(x2, norm_w, w_in_bf, layer, tabs, batch, seq, tm=DIFF_TK):
    m = x2.shape[0]
    tiles = seq // tm
    cos_t, sa_t, sb_t = tabs
    once = pl.Buffered(1)
    w_col = lambda c: pl.BlockSpec((None, D_MODEL, D_MODEL), lambda i: (layer, 0, c),
                                   pipeline_mode=once)
    tab_spec = pl.BlockSpec((tm, LANES), lambda i: (i % tiles, 0))
    a_shape = lambda d: jax.ShapeDtypeStruct((3, batch, d, seq // d, D_MODEL), _BF16)
    a_spec = lambda d: pl.BlockSpec((3, None, d, tm // d, D_MODEL),
                                    lambda i: (0, i // tiles, 0, i % tiles, 0))
    d1, d4, d16 = DILATIONS
    return pl.pallas_call(
        _proj_kernel,
        out_shape=(a_shape(d1), a_shape(d4), a_shape(d16),
                   jax.ShapeDtypeStruct((2, m, D_MODEL), _BF16),
                   jax.ShapeDtypeStruct((m // tm, D_MODEL, tm), _BF16)),
        grid=(m // tm,),
        in_specs=[
            pl.BlockSpec((tm, D_MODEL), lambda i: (i, 0)),
            pl.BlockSpec((None, 1, D_MODEL), lambda i: (layer, 0, 0)),
            w_col(0), w_col(1), w_col(2), w_col(4), w_col(5), w_col(6),
            tab_spec, tab_spec, tab_spec,
        ],
        out_specs=(pl.BlockSpec((3, None, None, tm, D_MODEL),
                                lambda i: (0, i // tiles, 0, i % tiles, 0)),
                   a_spec(d4), a_spec(d16),
                   pl.BlockSpec((2, tm, D_MODEL), lambda i: (0, i, 0)),
                   pl.BlockSpec((1, D_MODEL, tm), lambda i: (i, 0, 0))),
        scratch_shapes=[pltpu.VMEM((D_MODEL // LANES, tm, LANES), _F32)] * 2,
        compiler_params=pltpu.CompilerParams(
            dimension_semantics=("parallel",), vmem_limit_bytes=VMEM_LIMIT),
        name="qkv_proj",
    )(x2, norm_w, w_in_bf, w_in_bf, w_in_bf, w_in_bf, w_in_bf, w_in_bf, cos_t, sa_t, sb_t)


def _dilated_kernel(bias0_ref, bias_ref, q_ref, kp_ref, kc_ref, vp_ref, vc_ref, o_ref, st_ref):
    n_blocks = q_ref.shape[0] // BAND
    lane = lax.broadcasted_iota(jnp.int32, (BAND, LANES), 1)
    low = lane < HEAD_DIM
    ones = jnp.ones((2 * BAND, LANES), _BF16)

    def band_scores(u, hp):
        rows = slice(u * BAND, (u + 1) * BAND)
        sl = slice(hp * LANES, (hp + 1) * LANES)
        q = q_ref[rows, sl]
        zero = jnp.zeros_like(q)
        qq = jnp.concatenate([jnp.where(low, q, zero), jnp.where(low, zero, q)], axis=0)
        k_prev = kp_ref[:, sl] if u == 0 else kc_ref[(u - 1) * BAND:u * BAND, sl]
        kk = jnp.concatenate([k_prev, kc_ref[rows, sl]], axis=0)
        bias = (bias0_ref if u == 0 else bias_ref)[...]
        return lax.dot_general(qq, kk, (((1,), (1,)), ((), ())),
                               preferred_element_type=_F32) + bias

    work = [(u, hp) for u in range(n_blocks) for hp in range(A_HEADS // 2)]
    s = band_scores(*work[0])
    st = None
    for i, (u, hp) in enumerate(work):
        s_next = band_scores(*work[i + 1]) if i + 1 < len(work) else None
        rows = slice(u * BAND, (u + 1) * BAND)
        sl = slice(hp * LANES, (hp + 1) * LANES)
        if hp == 0:
            st = jnp.zeros((BAND, LANES), _F32)
        v_prev = vp_ref[:, sl] if u == 0 else vc_ref[(u - 1) * BAND:u * BAND, sl]
        va = jnp.concatenate([jnp.concatenate([v_prev, vc_ref[rows, sl]], axis=0), ones],
                             axis=1)
        m = jnp.max(s, axis=1, keepdims=True)
        p = jnp.exp(s - m).astype(_BF16)
        pv = jnp.dot(p, va, preferred_element_type=_F32)
        num, den = pv[:, :LANES], pv[:, LANES:]
        o_ref[rows, sl] = jnp.where(low, num[:BAND], num[BAND:])
        for e in range(2):
            h = 2 * hp + e
            st = jnp.where(lane == h, m[e * BAND:(e + 1) * BAND], st)
            st = jnp.where(lane == A_HEADS + h, den[e * BAND:(e + 1) * BAND], st)
        if hp == A_HEADS // 2 - 1:
            st_ref[rows, :] = st
        s = s_next


def _dilated_call(qkv, bias, dil, batch, seq):
    per_class = seq // dil
    qb = min(DIL_QB, per_class // BAND)
    rows = qb * BAND
    steps = per_class // rows
    cur = lambda which: pl.BlockSpec((None, None, None, rows, D_MODEL),
                                     lambda b, r, n: (which, b, r, n, 0))
    prev = lambda which: pl.BlockSpec(
        (None, None, None, BAND, D_MODEL),
        lambda b, r, n: (which, b, r, jnp.maximum(n * qb - 1, 0), 0))
    return pl.pallas_call(
        _dilated_kernel,
        out_shape=(jax.ShapeDtypeStruct((batch, dil, per_class, D_MODEL), _F32),
                   jax.ShapeDtypeStruct((batch, dil, per_class, LANES), _F32)),
        grid=(batch, dil, steps),
        in_specs=[
            pl.BlockSpec((None, 2 * BAND, 2 * BAND), lambda b, r, n: (jnp.minimum(n, 1), 0, 0)),
            pl.BlockSpec((None, 2 * BAND, 2 * BAND), lambda b, r, n: (1, 0, 0)),
            cur(0), prev(1), cur(1), prev(2), cur(2),
        ],
        out_specs=(pl.BlockSpec((None, None, rows, D_MODEL), lambda b, r, n: (b, r, n, 0)),
                   pl.BlockSpec((None, None, rows, LANES), lambda b, r, n: (b, r, n, 0))),
        compiler_params=pltpu.CompilerParams(
            dimension_semantics=("parallel", "parallel", "arbitrary"),
            vmem_limit_bytes=VMEM_LIMIT),
        name=f"dilated_d{dil}",
    )(bias, bias, qkv, qkv, qkv, qkv, qkv)


def _band_bias():
    qi = jnp.arange(BAND)[:, None]
    kc = jnp.arange(2 * BAND)[None, :]
    ok = (kc >= qi) & (kc <= qi + BAND)
    first = ok & (kc >= BAND)
    both = jnp.stack([first, ok]).astype(_F32)
    both = jnp.concatenate([both, both], axis=1)
    return (1.0 - both) * NEG


def _diff_kernel(lam_ref, sw_ref, q_ref, k_ref, vt_ref, o_ref, m_ref, acc_ref, *s_refs,
                 tq, tk, lam_init):
    qi = pl.program_id(2)
    lane = lax.broadcasted_iota(jnp.int32, (tq, LANES), 1)
    low = lane < HEAD_DIM
    q = q_ref[...]
    zero = jnp.zeros_like(q)
    qq = jnp.concatenate([jnp.where(low, q, zero), jnp.where(low, zero, q)], axis=0)
    ones = jnp.ones((ONES_ROWS, tk), _BF16)
    sub_blocks = tq // tk

    def chunks(lo):
        n = tq - lo
        return [(c, (lo + c if c < n else tq + lo + c - n), c % n)
                for c in range(0, 2 * n, DIFF_CHUNK)]

    def score_chunk(j, lo, idx):
        c, sc, _ = chunks(lo)[idx]
        start = j * tk if isinstance(j, int) else pl.multiple_of(j * tk, tk)
        s_refs[c // DIFF_CHUNK][...] = lax.dot_general(
            k_ref[pl.ds(start, tk), :], qq[sc:sc + DIFF_CHUNK], (((1,), (1,)), ((), ())),
            preferred_element_type=_F32)

    def update_chunk(j, diag, idx):
        lo = 0 if diag is None else diag * tk
        c, sc, q0 = chunks(lo)[idx]
        cols = slice(sc, sc + DIFF_CHUNK)
        s = s_refs[c // DIFF_CHUNK][...]
        if diag is not None and q0 < tk - 1:
            key = lax.broadcasted_iota(jnp.int32, s.shape, 0)
            qry = lax.broadcasted_iota(jnp.int32, s.shape, 1) + q0
            s = jnp.where(key <= qry, s, NEG)
        m = m_ref[:, cols]
        m_new = jnp.maximum(m, jnp.max(s, axis=0, keepdims=True))
        alpha = jnp.exp(m - m_new)
        p = jnp.exp(s - m_new).astype(_BF16)
        va = jnp.concatenate([vt_ref[j], ones], axis=0)
        acc_ref[:, cols] = alpha * acc_ref[:, cols] + jnp.dot(va, p, preferred_element_type=_F32)
        m_ref[:, cols] = m_new

    n_full = len(chunks(0))
    m_ref[...] = jnp.full(m_ref.shape, NEG, _F32)
    acc_ref[...] = jnp.zeros(acc_ref.shape, _F32)
    for idx in range(DIFF_LAG):
        score_chunk(0, 0, idx)

    def body(t, carry):
        for u in range(sub_blocks):
            j = sub_blocks * t + u
            for idx in range(n_full):
                ahead = idx + DIFF_LAG
                if ahead < n_full:
                    score_chunk(j, 0, ahead)
                else:
                    score_chunk(j + 1, 0, ahead - n_full)
                update_chunk(j, None, idx)
        return carry

    j0 = sub_blocks * qi
    lax.fori_loop(0, qi, body, 0)
    flat = [(d, idx) for d in range(sub_blocks) for idx in range(len(chunks(d * tk)))]
    for pos, (d, idx) in enumerate(flat):
        if pos + DIFF_LAG < len(flat):
            d_a, idx_a = flat[pos + DIFF_LAG]
            score_chunk(j0 + d_a, d_a * tk, idx_a)
        update_chunk(j0 + d, d, idx)


    acc = acc_ref[...]
    lp = lam_ref[...]
    lam = (jnp.exp(jnp.sum(lp[0:1] * lp[1:2], axis=1, keepdims=True))
           - jnp.exp(jnp.sum(lp[2:3] * lp[3:4], axis=1, keepdims=True)) + lam_init)
    num, den = acc[:LANES], acc[LANES:LANES + 1]
    ot = num[:, :tq] / den[:, :tq] - lam * (num[:, tq:] / den[:, tq:])
    o = ot.T
    o_ref[...] = _rms(o, sw_ref[...], SUBLN_EPS) * (1.0 - lam_init)


def _diff_call(qk, vt, lam_p, subln_w, layer, batch, seq, tq=DIFF_TQ, tk=DIFF_TK):
    assert tq % tk == 0 and (2 * tk) % DIFF_CHUNK == 0
    m = batch * seq
    nq = seq // tq
    nk = seq // tk
    kern = functools.partial(_diff_kernel, tq=tq, tk=tk, lam_init=_lambda_init(layer))
    return pl.pallas_call(
        kern,
        out_shape=jax.ShapeDtypeStruct((m, D_MODEL), _F32),
        grid=(batch, B_HEADS, nq),
        in_specs=[
            pl.BlockSpec((None, 4, HEAD_DIM), lambda b, h, i: (layer, 0, 0)),
            pl.BlockSpec((None, 1, LANES), lambda b, h, i: (layer, 0, 0)),
            pl.BlockSpec((None, tq, LANES), lambda b, h, i: (0, b * nq + i, h)),
            pl.BlockSpec((None, seq, LANES), lambda b, h, i: (1, b, h)),
            pl.BlockSpec((nk, LANES, tk), lambda b, h, i: (b, h, 0)),
        ],
        out_specs=pl.BlockSpec((tq, LANES), lambda b, h, i: (b * nq + i, h)),
        scratch_shapes=[pltpu.VMEM((1, 2 * tq), _F32),
                        pltpu.VMEM((LANES + ONES_ROWS, 2 * tq), _F32)]
                       + [pltpu.VMEM((tk, DIFF_CHUNK), _F32)] * (2 * tq // DIFF_CHUNK),
        compiler_params=pltpu.CompilerParams(
            dimension_semantics=("parallel", "parallel", "arbitrary"),
            vmem_limit_bytes=VMEM_LIMIT),
        name="diff_attn",
    )(lam_p, subln_w, qk, qk, vt)


def _expand_heads(w, e):
    hi = w.astype(_BF16)
    lo = (w - hi.astype(_F32)).astype(_BF16)
    return jnp.dot(jnp.concatenate([hi, lo], axis=1), e, preferred_element_type=_F32)


def _merge_kernel(x_ref, o1_ref, o2_ref, o3_ref, s1_ref, s2_ref, s3_ref, ob_ref, nw_ref,
                  wza_ref, wzb_ref, wga_ref, wgb_ref, wpa_ref, wpb_ref, wo_ref, e_ref, fw_ref,
                  out_ref, on2_ref, on3_ref, tmp_ref, sn2_ref, sn3_ref, *, final):
    x = x_ref[...]
    tm = x.shape[0]
    hdn = _rms(x, nw_ref[...], RMS_EPS).astype(_BF16)

    n_blk = D_MODEL // LANES
    d4, d16 = DILATIONS[1], DILATIONS[2]
    step = d16 // d4
    for blk in range(n_blk):
        sl = slice(blk * LANES, (blk + 1) * LANES)
        for r in range(d4):
            on2_ref[blk, pl.ds(r, tm // d4, stride=d4), :] = o2_ref[r, :, sl]
        for r in range(d16):
            c, b = r % d4, r // d4
            tmp_ref[blk, pl.ds(c * (tm // d4) + b, tm // d16, stride=step), :] = o3_ref[r, :, sl]
        for c in range(d4):
            on3_ref[blk, pl.ds(c, tm // d4, stride=d4), :] = (
                tmp_ref[blk, c * (tm // d4):(c + 1) * (tm // d4), :])
    for d, src, dst in ((DILATIONS[1], s2_ref, sn2_ref), (DILATIONS[2], s3_ref, sn3_ref)):
        for r in range(d):
            dst[pl.ds(r, tm // d, stride=d), :] = src[r]
    o1 = o1_ref[...]
    outs = ([o1[:, blk * LANES:(blk + 1) * LANES] for blk in range(n_blk)],
            [on2_ref[blk] for blk in range(n_blk)], [on3_ref[blk] for blk in range(n_blk)])

    sts = (s1_ref[...], sn2_ref[...], sn3_ref[...])
    lane = lax.broadcasted_iota(jnp.int32, sts[0].shape, 1)
    mx = jnp.maximum(jnp.maximum(sts[0], sts[1]), sts[2])
    wts = [jnp.exp(s - mx) for s in sts]
    dens = [pltpu.roll(s, LANES - A_HEADS, axis=1) * w for s, w in zip(sts, wts)]
    inv = 1.0 / (dens[0] + dens[1] + dens[2])
    e = e_ref[...]
    ya_blocks = None
    for w, o_blocks in zip(wts, outs):
        wn = _expand_heads(jnp.where(lane < A_HEADS, w * inv, 0.0), e)
        terms = [wn[:, blk * LANES:(blk + 1) * LANES] * o_blocks[blk] for blk in range(n_blk)]
        ya_blocks = terms if ya_blocks is None else [a + t for a, t in zip(ya_blocks, terms)]
    ya = jnp.concatenate(ya_blocks, axis=1)

    za = jnp.dot(hdn, wza_ref[...], preferred_element_type=_F32)
    ya = (ya * (za * _sigmoid(za))).astype(_BF16)
    pa = jnp.dot(ya, wpa_ref[...], preferred_element_type=_F32)
    ga = jnp.dot(hdn, wga_ref[...], preferred_element_type=_F32)
    merged = _sigmoid(ga) * pa

    zb = jnp.dot(hdn, wzb_ref[...], preferred_element_type=_F32)
    yb = (ob_ref[...] * (zb * _sigmoid(zb))).astype(_BF16)
    pb = jnp.dot(yb, wpb_ref[...], preferred_element_type=_F32)
    gb = jnp.dot(hdn, wgb_ref[...], preferred_element_type=_F32)
    merged = (merged + _sigmoid(gb) * pb).astype(_BF16)

    y = x + jnp.dot(merged, wo_ref[...], preferred_element_type=_F32)
    if final:
        y = _rms(y, fw_ref[...], RMS_EPS)
    out_ref[...] = y


def _merge_call(x2, os_, sts, ob, norm_w, w_in_bf, wpa_bf, wpb_bf, wo_bf, expand, final_w,
                layer, final, seq, tm=256):
    m = x2.shape[0]
    tiles = seq // tm
    row = lambda width: pl.BlockSpec((tm, width), lambda i: (i, 0))
    pat = lambda d, width: pl.BlockSpec((None, d, tm // d, width),
                                        lambda i: (i // tiles, 0, i % tiles, 0))
    d1, d4, d16 = DILATIONS
    o1 = os_[0].reshape(m, D_MODEL)
    s1 = sts[0].reshape(m, LANES)
    once = pl.Buffered(1)
    w_in_col = lambda c: pl.BlockSpec((None, D_MODEL, D_MODEL), lambda i: (layer, 0, c),
                                      pipeline_mode=once)
    w_sq = pl.BlockSpec((None, D_MODEL, D_MODEL), lambda i: (layer, 0, 0), pipeline_mode=once)
    return pl.pallas_call(
        functools.partial(_merge_kernel, final=final),
        out_shape=jax.ShapeDtypeStruct((m, D_MODEL), _F32),
        grid=(m // tm,),
        in_specs=[
            row(D_MODEL), row(D_MODEL), pat(d4, D_MODEL), pat(d16, D_MODEL),
            row(LANES), pat(d4, LANES), pat(d16, LANES), row(D_MODEL),
            pl.BlockSpec((None, 1, D_MODEL), lambda i: (layer, 0, 0)),
            w_in_col(3), w_in_col(7), w_in_col(8), w_in_col(9),
            w_sq, w_sq, w_sq,
            pl.BlockSpec((2 * LANES, D_MODEL), lambda i: (0, 0), pipeline_mode=once),
            pl.BlockSpec((1, D_MODEL), lambda i: (0, 0)),
        ],
        out_specs=row(D_MODEL),
        scratch_shapes=[pltpu.VMEM((D_MODEL // LANES, tm, LANES), _F32)] * 3
                       + [pltpu.VMEM((tm, LANES), _F32)] * 2,
        compiler_params=pltpu.CompilerParams(
            dimension_semantics=("parallel",), vmem_limit_bytes=VMEM_LIMIT),
        name="merge_out",
    )(x2, o1, os_[1], os_[2], s1, sts[1], sts[2], ob, norm_w, w_in_bf, w_in_bf, w_in_bf, w_in_bf,
      wpa_bf, wpb_bf, wo_bf, expand, final_w)


def _rope_tables(seq):
    half = ROPE_DIM // 2
    inv = 1.0 / (ROPE_THETA ** (jnp.arange(0, ROPE_DIM, 2, dtype=_F32) / ROPE_DIM))
    ang = jnp.arange(seq, dtype=_F32)[:, None] * inv[None, :]
    cos, sin = jnp.cos(ang), jnp.sin(ang)
    rest = HEAD_DIM - ROPE_DIM
    z = lambda n: jnp.zeros((seq, n), _F32)
    cos_h = jnp.concatenate([cos, cos, jnp.ones((seq, rest), _F32)], axis=1)
    sa_h = jnp.concatenate([z(half), sin, z(rest)], axis=1)
    sb_h = jnp.concatenate([-sin, z(half), z(rest)], axis=1)
    rep = LANES // HEAD_DIM
    return tuple(jnp.tile(t, (1, rep)) for t in (cos_h, sa_h, sb_h))


def kernel(x, norm_w, w_in, lambda_q1, lambda_k1, lambda_q2, lambda_k2, subln_w,
           w_proj_a, w_proj_b, w_out, final_norm_w):
    batch, seq, d = x.shape
    depth = norm_w.shape[0]
    assert d == D_MODEL and seq % (BAND * DILATIONS[-1]) == 0
    assert w_in.shape[-1] == N_COL_BLOCKS * D_MODEL

    w_in_bf = w_in.astype(_BF16)
    wpa_bf = w_proj_a.astype(_BF16)
    wpb_bf = w_proj_b.astype(_BF16)
    wo_bf = w_out.astype(_BF16)
    lam_p = jnp.stack([lambda_q1, lambda_k1, lambda_q2, lambda_k2], axis=1)
    tabs = _rope_tables(seq)
    bias = _band_bias()
    head_of_col = jnp.arange(D_MODEL)[None, :] // HEAD_DIM
    expand = (jnp.arange(2 * LANES)[:, None] % LANES == head_of_col).astype(_BF16)
    final_w = final_norm_w.reshape(1, D_MODEL)
    norm_w = norm_w.reshape(depth, 1, D_MODEL)
    subln_w = subln_w.reshape(depth, 1, LANES)

    h = x.reshape(batch * seq, D_MODEL)
    for layer in range(depth):
        *a_layouts, qk, vt = _proj_call(h, norm_w, w_in_bf, layer, tabs, batch, seq)
        os_, sts = zip(*[_dilated_call(a, bias, dil, batch, seq)
                         for a, dil in zip(a_layouts, DILATIONS)])
        ob = _diff_call(qk, vt, lam_p, subln_w, layer, batch, seq)
        h = _merge_call(h, os_, sts, ob, norm_w, w_in_bf, wpa_bf, wpb_bf, wo_bf, expand,
                        final_w, layer, layer == depth - 1, seq)
    return h.reshape(batch, seq, D_MODEL)
```

```python
import functools
import math

import jax
import jax.numpy as jnp
from jax import lax
from jax.experimental import pallas as pl
from jax.experimental.pallas import tpu as pltpu

D_MODEL = 1024
HEAD_DIM = 64
A_HEADS = 16
B_HEADS = 8
DILATIONS = (1, 4, 16)
BAND = 128
ROPE_THETA = 500000.0
ROPE_DIM = HEAD_DIM // 4
RMS_EPS = 1e-6
SUBLN_EPS = 1e-5
NEG = -1e30
LANES = 128
N_COL_BLOCKS = 10
VMEM_LIMIT = 56 * 1024 * 1024
DIFF_TQ = 2048
DIL_QB = 8
DIFF_TK = 512
DIFF_CHUNK = 256
DIFF_LAG = 7
ONES_ROWS = 16

_F32 = jnp.float32
_BF16 = jnp.bfloat16


def _lambda_init(layer):
    return 0.8 - 0.6 * math.exp(-0.3 * layer)


def _rms(x, w, eps):
    return x * lax.rsqrt(jnp.mean(x * x, axis=-1, keepdims=True) + eps) * w


def _sigmoid(z):
    return 1.0 / (1.0 + jnp.exp(-z))


def _proj_kernel(x_ref, nw_ref, wqa_ref, wka_ref, wva_ref, wqb_ref, wkb_ref, wvb_ref,
                 cos_ref, sa_ref, sb_ref, a1_ref, a4_ref, a16_ref, qk_ref, vt_ref,
                 scr0_ref, scr1_ref):
    tm = x_ref.shape[0]
    hdn = _rms(x_ref[...], nw_ref[...], RMS_EPS).astype(_BF16)
    cos, sa, sb = cos_ref[...], sa_ref[...], sb_ref[...]
    q_scale = HEAD_DIM ** -0.5
    tabs = {True: (cos * q_scale, sa * q_scale, sb * q_scale), False: (cos, sa, sb)}

    def lane_blocks(u):
        return [u[:, blk * LANES:(blk + 1) * LANES] for blk in range(D_MODEL // LANES)]

    def rope(u, is_q):
        c, s_a, s_b = tabs[is_q]
        return [t * c + pltpu.roll(t, ROPE_DIM // 2, axis=1) * s_a
                + pltpu.roll(t, LANES - ROPE_DIM // 2, axis=1) * s_b for t in lane_blocks(u)]

    d4, d16 = DILATIONS[1], DILATIONS[2]
    step = d16 // d4

    def residue_layouts(blocks, which):
        for blk, t in enumerate(blocks):
            sl = slice(blk * LANES, (blk + 1) * LANES)
            a1_ref[which, :, sl] = t.astype(_BF16)
            scr0_ref[blk] = t
            for c in range(d4):
                x = scr0_ref[blk, pl.ds(c, tm // d4, stride=d4), :]
                a4_ref[which, c, :, sl] = x.astype(_BF16)
                scr1_ref[blk, c * (tm // d4):(c + 1) * (tm // d4), :] = x
            for r in range(d16):
                c, b = r % d4, r // d4
                x = scr1_ref[blk, pl.ds(c * (tm // d4) + b, tm // d16, stride=step), :]
                a16_ref[which, r, :, sl] = x.astype(_BF16)

    def natural(ref, blocks, which):
        for blk, t in enumerate(blocks):
            ref[which, :, blk * LANES:(blk + 1) * LANES] = t.astype(_BF16)

    dot = lambda w_ref: jnp.dot(hdn, w_ref[...], preferred_element_type=_F32)
    residue_layouts(rope(dot(wqa_ref), True), 0)
    residue_layouts(rope(dot(wka_ref), False), 1)
    residue_layouts(lane_blocks(dot(wva_ref)), 2)
    natural(qk_ref, rope(dot(wqb_ref), True), 0)
    natural(qk_ref, rope(dot(wkb_ref), False), 1)
    vt_ref[0] = dot(wvb_ref).T.astype(_BF16)


def _proj_call(x2, norm_w, w_in_bf, layer, tabs, batch, seq, tm=DIFF_TK):
    m = x2.shape[0]
    tiles = seq // tm
    cos_t, sa_t, sb_t = tabs
    once = pl.Buffered(1)
    w_col = lambda c: pl.BlockSpec((None, D_MODEL, D_MODEL), lambda i: (layer, 0, c),
                                   pipeline_mode=once)
    tab_spec = pl.BlockSpec((tm, LANES), lambda i: (i % tiles, 0))
    a_shape = lambda d: jax.ShapeDtypeStruct((3, batch, d, seq // d, D_MODEL), _BF16)
    a_spec = lambda d: pl.BlockSpec((3, None, d, tm // d, D_MODEL),
                                    lambda i: (0, i // tiles, 0, i % tiles, 0))
    d1, d4, d16 = DILATIONS
    return pl.pallas_call(
        _proj_kernel,
        out_shape=(a_shape(d1), a_shape(d4), a_shape(d16),
                   jax.ShapeDtypeStruct((2, m, D_MODEL), _BF16),
                   jax.ShapeDtypeStruct((m // tm, D_MODEL, tm), _BF16)),
        grid=(m // tm,),
        in_specs=[
            pl.BlockSpec((tm, D_MODEL), lambda i: (i, 0)),
            pl.BlockSpec((None, 1, D_MODEL), lambda i: (layer, 0, 0)),
            w_col(0), w_col(1), w_col(2), w_col(4), w_col(5), w_col(6),
            tab_spec, tab_spec, tab_spec,
        ],
        out_specs=(pl.BlockSpec((3, None, None, tm, D_MODEL),
                                lambda i: (0, i // tiles, 0, i % tiles, 0)),
                   a_spec(d4), a_spec(d16),
                   pl.BlockSpec((2, tm, D_MODEL), lambda i: (0, i, 0)),
                   pl.BlockSpec((1, D_MODEL, tm), lambda i: (i, 0, 0))),
        scratch_shapes=[pltpu.VMEM((D_MODEL // LANES, tm, LANES), _F32)] * 2,
        compiler_params=pltpu.CompilerParams(
            dimension_semantics=("parallel",), vmem_limit_bytes=VMEM_LIMIT),
        name="qkv_proj",
    )(x2, norm_w, w_in_bf, w_in_bf, w_in_bf, w_in_bf, w_in_bf, w_in_bf, cos_t, sa_t, sb_t)


def _dilated_kernel(bias0_ref, bias_ref, q_ref, kp_ref, kc_ref, vp_ref, vc_ref, o_ref, st_ref):
    n_blocks = q_ref.shape[0] // BAND
    lane = lax.broadcasted_iota(jnp.int32, (BAND, LANES), 1)
    low = lane < HEAD_DIM
    ones = jnp.ones((2 * BAND, LANES), _BF16)

    def band_scores(u, hp):
        rows = slice(u * BAND, (u + 1) * BAND)
        sl = slice(hp * LANES, (hp + 1) * LANES)
        q = q_ref[rows, sl]
        zero = jnp.zeros_like(q)
        qq = jnp.concatenate([jnp.where(low, q, zero), jnp.where(low, zero, q)], axis=0)
        k_prev = kp_ref[:, sl] if u == 0 else kc_ref[(u - 1) * BAND:u * BAND, sl]
        kk = jnp.concatenate([k_prev, kc_ref[rows, sl]], axis=0)
        bias = (bias0_ref if u == 0 else bias_ref)[...]
        return lax.dot_general(qq, kk, (((1,), (1,)), ((), ())),
                               preferred_element_type=_F32) + bias

    work = [(u, hp) for u in range(n_blocks) for hp in range(A_HEADS // 2)]
    s = band_scores(*work[0])
    st = None
    for i, (u, hp) in enumerate(work):
        s_next = band_scores(*work[i + 1]) if i + 1 < len(work) else None
        rows = slice(u * BAND, (u + 1) * BAND)
        sl = slice(hp * LANES, (hp + 1) * LANES)
        if hp == 0:
            st = jnp.zeros((BAND, LANES), _F32)
        v_prev = vp_ref[:, sl] if u == 0 else vc_ref[(u - 1) * BAND:u * BAND, sl]
        va = jnp.concatenate([jnp.concatenate([v_prev, vc_ref[rows, sl]], axis=0), ones],
                             axis=1)
        m = jnp.max(s, axis=1, keepdims=True)
        p = jnp.exp(s - m).astype(_BF16)
        pv = jnp.dot(p, va, preferred_element_type=_F32)
        num, den = pv[:, :LANES], pv[:, LANES:]
        o_ref[rows, sl] = jnp.where(low, num[:BAND], num[BAND:])
        for e in range(2):
            h = 2 * hp + e
            st = jnp.where(lane == h, m[e * BAND:(e + 1) * BAND], st)
            st = jnp.where(lane == A_HEADS + h, den[e * BAND:(e + 1) * BAND], st)
        if hp == A_HEADS // 2 - 1:
            st_ref[rows, :] = st
        s = s_next


def _dilated_call(qkv, bias, dil, batch, seq):
    per_class = seq // dil
    qb = min(DIL_QB, per_class // BAND)
    rows = qb * BAND
    steps = per_class // rows
    cur = lambda which: pl.BlockSpec((None, None, None, rows, D_MODEL),
                                     lambda b, r, n: (which, b, r, n, 0))
    prev = lambda which: pl.BlockSpec(
        (None, None, None, BAND, D_MODEL),
        lambda b, r, n: (which, b, r, jnp.maximum(n * qb - 1, 0), 0))
    return pl.pallas_call(
        _dilated_kernel,
        out_shape=(jax.ShapeDtypeStruct((batch, dil, per_class, D_MODEL), _F32),
                   jax.ShapeDtypeStruct((batch, dil, per_class, LANES), _F32)),
        grid=(batch, dil, steps),
        in_specs=[
            pl.BlockSpec((None, 2 * BAND, 2 * BAND), lambda b, r, n: (jnp.minimum(n, 1), 0, 0)),
            pl.BlockSpec((None, 2 * BAND, 2 * BAND), lambda b, r, n: (1, 0, 0)),
            cur(0), prev(1), cur(1), prev(2), cur(2),
        ],
        out_specs=(pl.BlockSpec((None, None, rows, D_MODEL), lambda b, r, n: (b, r, n, 0)),
                   pl.BlockSpec((None, None, rows, LANES), lambda b, r, n: (b, r, n, 0))),
        compiler_params=pltpu.CompilerParams(
            dimension_semantics=("parallel", "parallel", "arbitrary"),
            vmem_limit_bytes=VMEM_LIMIT),
        name=f"dilated_d{dil}",
    )(bias, bias, qkv, qkv, qkv, qkv, qkv)


def _band_bias():
    qi = jnp.arange(BAND)[:, None]
    kc = jnp.arange(2 * BAND)[None, :]
    ok = (kc >= qi) & (kc <= qi + BAND)
    first = ok & (kc >= BAND)
    both = jnp.stack([first, ok]).astype(_F32)
    both = jnp.concatenate([both, both], axis=1)
    return (1.0 - both) * NEG


def _diff_kernel(lam_ref, sw_ref, q_ref, k_ref, vt_ref, o_ref, m_ref, acc_ref, *s_refs,
                 tq, tk, lam_init):
    qi = pl.program_id(2)
    lane = lax.broadcasted_iota(jnp.int32, (tq, LANES), 1)
    low = lane < HEAD_DIM
    q = q_ref[...]
    zero = jnp.zeros_like(q)
    qq = jnp.concatenate([jnp.where(low, q, zero), jnp.where(low, zero, q)], axis=0)
    ones = jnp.ones((ONES_ROWS, tk), _BF16)
    sub_blocks = tq // tk

    def chunks(lo):
        n = tq - lo
        return [(c, (lo + c if c < n else tq + lo + c - n), c % n)
                for c in range(0, 2 * n, DIFF_CHUNK)]

    def score_chunk(j, lo, idx):
        c, sc, _ = chunks(lo)[idx]
        start = j * tk if isinstance(j, int) else pl.multiple_of(j * tk, tk)
        s_refs[c // DIFF_CHUNK][...] = lax.dot_general(
            k_ref[pl.ds(start, tk), :], qq[sc:sc + DIFF_CHUNK], (((1,), (1,)), ((), ())),
            preferred_element_type=_F32)

    def update_chunk(j, diag, idx):
        lo = 0 if diag is None else diag * tk
        c, sc, q0 = chunks(lo)[idx]
        cols = slice(sc, sc + DIFF_CHUNK)
        s = s_refs[c // DIFF_CHUNK][...]
        if diag is not None and q0 < tk - 1:
            key = lax.broadcasted_iota(jnp.int32, s.shape, 0)
            qry = lax.broadcasted_iota(jnp.int32, s.shape, 1) + q0
            s = jnp.where(key <= qry, s, NEG)
        m = m_ref[:, cols]
        m_new = jnp.maximum(m, jnp.max(s, axis=0, keepdims=True))
        alpha = jnp.exp(m - m_new)
        p = jnp.exp(s - m_new).astype(_BF16)
        va = jnp.concatenate([vt_ref[j], ones], axis=0)
        acc_ref[:, cols] = alpha * acc_ref[:, cols] + jnp.dot(va, p, preferred_element_type=_F32)
        m_ref[:, cols] = m_new

    n_full = len(chunks(0))
    m_ref[...] = jnp.full(m_ref.shape, NEG, _F32)
    acc_ref[...] = jnp.zeros(acc_ref.shape, _F32)
    for idx in range(DIFF_LAG):
        score_chunk(0, 0, idx)

    def body(t, carry):
        for u in range(sub_blocks):
            j = sub_blocks * t + u
            for idx in range(n_full):
                ahead = idx + DIFF_LAG
                if ahead < n_full:
                    score_chunk(j, 0, ahead)
                else:
                    score_chunk(j + 1, 0, ahead - n_full)
                update_chunk(j, None, idx)
        return carry

    j0 = sub_blocks * qi
    lax.fori_loop(0, qi, body, 0)
    flat = [(d, idx) for d in range(sub_blocks) for idx in range(len(chunks(d * tk)))]
    for pos, (d, idx) in enumerate(flat):
        if pos + DIFF_LAG < len(flat):
            d_a, idx_a = flat[pos + DIFF_LAG]
            score_chunk(j0 + d_a, d_a * tk, idx_a)
        update_chunk(j0 + d, d, idx)


    acc = acc_ref[...]
    lp = lam_ref[...]
    lam = (jnp.exp(jnp.sum(lp[0:1] * lp[1:2], axis=1, keepdims=True))
           - jnp.exp(jnp.sum(lp[2:3] * lp[3:4], axis=1, keepdims=True)) + lam_init)
    num, den = acc[:LANES], acc[LANES:LANES + 1]
    ot = num[:, :tq] / den[:, :tq] - lam * (num[:, tq:] / den[:, tq:])
    o = ot.T
    o_ref[...] = _rms(o, sw_ref[...], SUBLN_EPS) * (1.0 - lam_init)


def _diff_call(qk, vt, lam_p, subln_w, layer, batch, seq, tq=DIFF_TQ, tk=DIFF_TK):
    assert tq % tk == 0 and (2 * tk) % DIFF_CHUNK == 0
    assert DIFF_LAG < 2 * min(tq, 2 * tk) // DIFF_CHUNK
    m = batch * seq
    nq = seq // tq
    nk = seq // tk
    kern = functools.partial(_diff_kernel, tq=tq, tk=tk, lam_init=_lambda_init(layer))
    return pl.pallas_call(
        kern,
        out_shape=jax.ShapeDtypeStruct((m, D_MODEL), _F32),
        grid=(batch, B_HEADS, nq),
        in_specs=[
            pl.BlockSpec((None, 4, HEAD_DIM), lambda b, h, i: (layer, 0, 0)),
            pl.BlockSpec((None, 1, LANES), lambda b, h, i: (layer, 0, 0)),
            pl.BlockSpec((None, tq, LANES), lambda b, h, i: (0, b * nq + i, h)),
            pl.BlockSpec((None, seq, LANES), lambda b, h, i: (1, b, h)),
            pl.BlockSpec((nk, LANES, tk), lambda b, h, i: (b, h, 0)),
        ],
        out_specs=pl.BlockSpec((tq, LANES), lambda b, h, i: (b * nq + i, h)),
        scratch_shapes=[pltpu.VMEM((1, 2 * tq), _F32),
                        pltpu.VMEM((LANES + ONES_ROWS, 2 * tq), _F32)]
                       + [pltpu.VMEM((tk, DIFF_CHUNK), _F32)] * (2 * tq // DIFF_CHUNK),
        compiler_params=pltpu.CompilerParams(
            dimension_semantics=("parallel", "parallel", "arbitrary"),
            vmem_limit_bytes=VMEM_LIMIT),
        name="diff_attn",
    )(lam_p, subln_w, qk, qk, vt)


def _expand_heads(w, e):
    hi = w.astype(_BF16)
    lo = (w - hi.astype(_F32)).astype(_BF16)
    return jnp.dot(jnp.concatenate([hi, lo], axis=1), e, preferred_element_type=_F32)


def _merge_kernel(x_ref, o1_ref, o2_ref, o3_ref, s1_ref, s2_ref, s3_ref, ob_ref, nw_ref,
                  wza_ref, wzb_ref, wga_ref, wgb_ref, wpa_ref, wpb_ref, wo_ref, e_ref, fw_ref,
                  out_ref, on2_ref, on3_ref, tmp_ref, sn2_ref, sn3_ref, *, final):
    x = x_ref[...]
    tm = x.shape[0]
    hdn = _rms(x, nw_ref[...], RMS_EPS).astype(_BF16)

    n_blk = D_MODEL // LANES
    d4, d16 = DILATIONS[1], DILATIONS[2]
    step = d16 // d4
    for blk in range(n_blk):
        sl = slice(blk * LANES, (blk + 1) * LANES)
        for r in range(d4):
            on2_ref[blk, pl.ds(r, tm // d4, stride=d4), :] = o2_ref[r, :, sl]
        for r in range(d16):
            c, b = r % d4, r // d4
            tmp_ref[blk, pl.ds(c * (tm // d4) + b, tm // d16, stride=step), :] = o3_ref[r, :, sl]
        for c in range(d4):
            on3_ref[blk, pl.ds(c, tm // d4, stride=d4), :] = (
                tmp_ref[blk, c * (tm // d4):(c + 1) * (tm // d4), :])
    for d, src, dst in ((DILATIONS[1], s2_ref, sn2_ref), (DILATIONS[2], s3_ref, sn3_ref)):
        for r in range(d):
            dst[pl.ds(r, tm // d, stride=d), :] = src[r]
    o1 = o1_ref[...]
    outs = ([o1[:, blk * LANES:(blk + 1) * LANES] for blk in range(n_blk)],
            [on2_ref[blk] for blk in range(n_blk)], [on3_ref[blk] for blk in range(n_blk)])

    sts = (s1_ref[...], sn2_ref[...], sn3_ref[...])
    lane = lax.broadcasted_iota(jnp.int32, sts[0].shape, 1)
    mx = jnp.maximum(jnp.maximum(sts[0], sts[1]), sts[2])
    wts = [jnp.exp(s - mx) for s in sts]
    dens = [pltpu.roll(s, LANES - A_HEADS, axis=1) * w for s, w in zip(sts, wts)]
    inv = 1.0 / (dens[0] + dens[1] + dens[2])
    e = e_ref[...]
    ya_blocks = None
    for w, o_blocks in zip(wts, outs):
        wn = _expand_heads(jnp.where(lane < A_HEADS, w * inv, 0.0), e)
        terms = [wn[:, blk * LANES:(blk + 1) * LANES] * o_blocks[blk] for blk in range(n_blk)]
        ya_blocks = terms if ya_blocks is None else [a + t for a, t in zip(ya_blocks, terms)]
    ya = jnp.concatenate(ya_blocks, axis=1)

    za = jnp.dot(hdn, wza_ref[...], preferred_element_type=_F32)
    ya = (ya * (za * _sigmoid(za))).astype(_BF16)
    pa = jnp.dot(ya, wpa_ref[...], preferred_element_type=_F32)
    ga = jnp.dot(hdn, wga_ref[...], preferred_element_type=_F32)
    merged = _sigmoid(ga) * pa

    zb = jnp.dot(hdn, wzb_ref[...], preferred_element_type=_F32)
    yb = (ob_ref[...] * (zb * _sigmoid(zb))).astype(_BF16)
    pb = jnp.dot(yb, wpb_ref[...], preferred_element_type=_F32)
    gb = jnp.dot(hdn, wgb_ref[...], preferred_element_type=_F32)
    merged = (merged + _sigmoid(gb) * pb).astype(_BF16)

    y = x + jnp.dot(merged, wo_ref[...], preferred_element_type=_F32)
    if final:
        y = _rms(y, fw_ref[...], RMS_EPS)
    out_ref[...] = y


def _merge_call(x2, os_, sts, ob, norm_w, w_in_bf, wpa_bf, wpb_bf, wo_bf, expand, final_w,
                layer, final, seq, tm=256):
    m = x2.shape[0]
    tiles = seq // tm
    row = lambda width: pl.BlockSpec((tm, width), lambda i: (i, 0))
    pat = lambda d, width: pl.BlockSpec((None, d, tm // d, width),
                                        lambda i: (i // tiles, 0, i % tiles, 0))
    d1, d4, d16 = DILATIONS
    o1 = os_[0].reshape(m, D_MODEL)
    s1 = sts[0].reshape(m, LANES)
    once = pl.Buffered(1)
    w_in_col = lambda c: pl.BlockSpec((None, D_MODEL, D_MODEL), lambda i: (layer, 0, c),
                                      pipeline_mode=once)
    w_sq = pl.BlockSpec((None, D_MODEL, D_MODEL), lambda i: (layer, 0, 0), pipeline_mode=once)
    return pl.pallas_call(
        functools.partial(_merge_kernel, final=final),
        out_shape=jax.ShapeDtypeStruct((m, D_MODEL), _F32),
        grid=(m // tm,),
        in_specs=[
            row(D_MODEL), row(D_MODEL), pat(d4, D_MODEL), pat(d16, D_MODEL),
            row(LANES), pat(d4, LANES), pat(d16, LANES), row(D_MODEL),
            pl.BlockSpec((None, 1, D_MODEL), lambda i: (layer, 0, 0)),
            w_in_col(3), w_in_col(7), w_in_col(8), w_in_col(9),
            w_sq, w_sq, w_sq,
            pl.BlockSpec((2 * LANES, D_MODEL), lambda i: (0, 0), pipeline_mode=once),
            pl.BlockSpec((1, D_MODEL), lambda i: (0, 0)),
        ],
        out_specs=row(D_MODEL),
        scratch_shapes=[pltpu.VMEM((D_MODEL // LANES, tm, LANES), _F32)] * 3
                       + [pltpu.VMEM((tm, LANES), _F32)] * 2,
        compiler_params=pltpu.CompilerParams(
            dimension_semantics=("parallel",), vmem_limit_bytes=VMEM_LIMIT),
        name="merge_out",
    )(x2, o1, os_[1], os_[2], s1, sts[1], sts[2], ob, norm_w, w_in_bf, w_in_bf, w_in_bf, w_in_bf,
      wpa_bf, wpb_bf, wo_bf, expand, final_w)


def _rope_tables(seq):
    half = ROPE_DIM // 2
    inv = 1.0 / (ROPE_THETA ** (jnp.arange(0, ROPE_DIM, 2, dtype=_F32) / ROPE_DIM))
    ang = jnp.arange(seq, dtype=_F32)[:, None] * inv[None, :]
    cos, sin = jnp.cos(ang), jnp.sin(ang)
    rest = HEAD_DIM - ROPE_DIM
    z = lambda n: jnp.zeros((seq, n), _F32)
    cos_h = jnp.concatenate([cos, cos, jnp.ones((seq, rest), _F32)], axis=1)
    sa_h = jnp.concatenate([z(half), sin, z(rest)], axis=1)
    sb_h = jnp.concatenate([-sin, z(half), z(rest)], axis=1)
    rep = LANES // HEAD_DIM
    return tuple(jnp.tile(t, (1, rep)) for t in (cos_h, sa_h, sb_h))


def kernel(x, norm_w, w_in, lambda_q1, lambda_k1, lambda_q2, lambda_k2, subln_w,
           w_proj_a, w_proj_b, w_out, final_norm_w):
    batch, seq, d = x.shape
    depth = norm_w.shape[0]
    assert d == D_MODEL and seq % (BAND * DILATIONS[-1]) == 0
    assert w_in.shape[-1] == N_COL_BLOCKS * D_MODEL

    w_in_bf = w_in.astype(_BF16)
    wpa_bf = w_proj_a.astype(_BF16)
    wpb_bf = w_proj_b.astype(_BF16)
    wo_bf = w_out.astype(_BF16)
    lam_p = jnp.stack([lambda_q1, lambda_k1, lambda_q2, lambda_k2], axis=1)
    tabs = _rope_tables(seq)
    bias = _band_bias()
    head_of_col = jnp.arange(D_MODEL)[None, :] // HEAD_DIM
    expand = (jnp.arange(2 * LANES)[:, None] % LANES == head_of_col).astype(_BF16)
    final_w = final_norm_w.reshape(1, D_MODEL)
    norm_w = norm_w.reshape(depth, 1, D_MODEL)
    subln_w = subln_w.reshape(depth, 1, LANES)

    h = x.reshape(batch * seq, D_MODEL)
    for layer in range(depth):
        *a_layouts, qk, vt = _proj_call(h, norm_w, w_in_bf, layer, tabs, batch, seq)
        os_, sts = zip(*[_dilated_call(a, bias, dil, batch, seq)
                         for a, dil in zip(a_layouts, DILATIONS)])
        ob = _diff_call(qk, vt, lam_p, subln_w, layer, batch, seq)
        h = _merge_call(h, os_, sts, ob, norm_w, w_in_bf, wpa_bf, wpb_bf, wo_bf, expand,
                        final_w, layer, layer == depth - 1, seq)
    return h.reshape(batch, seq, D_MODEL)
```

```python
import functools
import math

import jax
import jax.numpy as jnp
from jax import lax
from jax.experimental import pallas as pl
from jax.experimental.pallas import tpu as pltpu

D_MODEL = 1024
HEAD_DIM = 64
A_HEADS = 16
B_HEADS = 8
DILATIONS = (1, 4, 16)
BAND = 128
ROPE_THETA = 500000.0
ROPE_DIM = HEAD_DIM // 4
RMS_EPS = 1e-6
SUBLN_EPS = 1e-5
NEG = -1e30
LANES = 128
N_COL_BLOCKS = 10
VMEM_LIMIT = 56 * 1024 * 1024
DIFF_TQ = 2048
DIL_QB = 8
DIFF_TK = 512
DIFF_CHUNK = 256
DIFF_LAG = 7
ONES_ROWS = 16

_F32 = jnp.float32
_BF16 = jnp.bfloat16


def _lambda_init(layer):
    return 0.8 - 0.6 * math.exp(-0.3 * layer)


def _rms(x, w, eps):
    return x * lax.rsqrt(jnp.mean(x * x, axis=-1, keepdims=True) + eps) * w


def _sigmoid(z):
    return 1.0 / (1.0 + jnp.exp(-z))


def _proj_kernel(x_ref, nw_ref, wqa_ref, wka_ref, wva_ref, wqb_ref, wkb_ref, wvb_ref,
                 cos_ref, sa_ref, sb_ref, a1_ref, a4_ref, a16_ref, qk_ref, vt_ref,
                 scr0_ref, scr1_ref):
    tm = x_ref.shape[0]
    hdn = _rms(x_ref[...], nw_ref[...], RMS_EPS).astype(_BF16)
    cos, sa, sb = cos_ref[...], sa_ref[...], sb_ref[...]
    q_scale = HEAD_DIM ** -0.5 * math.log2(math.e)
    tabs = {True: (cos * q_scale, sa * q_scale, sb * q_scale), False: (cos, sa, sb)}

    def lane_blocks(u):
        return [u[:, blk * LANES:(blk + 1) * LANES] for blk in range(D_MODEL // LANES)]

    def rope(u, is_q):
        c, s_a, s_b = tabs[is_q]
        return [t * c + pltpu.roll(t, ROPE_DIM // 2, axis=1) * s_a
                + pltpu.roll(t, LANES - ROPE_DIM // 2, axis=1) * s_b for t in lane_blocks(u)]

    d4, d16 = DILATIONS[1], DILATIONS[2]
    step = d16 // d4

    def residue_layouts(blocks, which):
        for blk, t in enumerate(blocks):
            sl = slice(blk * LANES, (blk + 1) * LANES)
            a1_ref[which, :, sl] = t.astype(_BF16)
            scr0_ref[blk] = t
            for c in range(d4):
                x = scr0_ref[blk, pl.ds(c, tm // d4, stride=d4), :]
                a4_ref[which, c, :, sl] = x.astype(_BF16)
                scr1_ref[blk, c * (tm // d4):(c + 1) * (tm // d4), :] = x
            for r in range(d16):
                c, b = r % d4, r // d4
                x = scr1_ref[blk, pl.ds(c * (tm // d4) + b, tm // d16, stride=step), :]
                a16_ref[which, r, :, sl] = x.astype(_BF16)

    def natural(ref, blocks, which):
        for blk, t in enumerate(blocks):
            ref[which, :, blk * LANES:(blk + 1) * LANES] = t.astype(_BF16)

    dot = lambda w_ref: jnp.dot(hdn, w_ref[...], preferred_element_type=_F32)
    residue_layouts(rope(dot(wqa_ref), True), 0)
    residue_layouts(rope(dot(wka_ref), False), 1)
    residue_layouts(lane_blocks(dot(wva_ref)), 2)
    natural(qk_ref, rope(dot(wqb_ref), True), 0)
    natural(qk_ref, rope(dot(wkb_ref), False), 1)
    vt_ref[0] = dot(wvb_ref).T.astype(_BF16)


def _proj_call(x2, norm_w, w_in_bf, layer, tabs, batch, seq, tm=DIFF_TK):
    m = x2.shape[0]
    tiles = seq // tm
    cos_t, sa_t, sb_t = tabs
    once = pl.Buffered(1)
    w_col = lambda c: pl.BlockSpec((None, D_MODEL, D_MODEL), lambda i: (layer, 0, c),
                                   pipeline_mode=once)
    tab_spec = pl.BlockSpec((tm, LANES), lambda i: (i % tiles, 0))
    a_shape = lambda d: jax.ShapeDtypeStruct((3, batch, d, seq // d, D_MODEL), _BF16)
    a_spec = lambda d: pl.BlockSpec((3, None, d, tm // d, D_MODEL),
                                    lambda i: (0, i // tiles, 0, i % tiles, 0))
    d1, d4, d16 = DILATIONS
    return pl.pallas_call(
        _proj_kernel,
        out_shape=(a_shape(d1), a_shape(d4), a_shape(d16),
                   jax.ShapeDtypeStruct((2, m, D_MODEL), _BF16),
                   jax.ShapeDtypeStruct((m // tm, D_MODEL, tm), _BF16)),
        grid=(m // tm,),
        in_specs=[
            pl.BlockSpec((tm, D_MODEL), lambda i: (i, 0)),
            pl.BlockSpec((None, 1, D_MODEL), lambda i: (layer, 0, 0)),
            w_col(0), w_col(1), w_col(2), w_col(4), w_col(5), w_col(6),
            tab_spec, tab_spec, tab_spec,
        ],
        out_specs=(pl.BlockSpec((3, None, None, tm, D_MODEL),
                                lambda i: (0, i // tiles, 0, i % tiles, 0)),
                   a_spec(d4), a_spec(d16),
                   pl.BlockSpec((2, tm, D_MODEL), lambda i: (0, i, 0)),
                   pl.BlockSpec((1, D_MODEL, tm), lambda i: (i, 0, 0))),
        scratch_shapes=[pltpu.VMEM((D_MODEL // LANES, tm, LANES), _F32)] * 2,
        compiler_params=pltpu.CompilerParams(
            dimension_semantics=("parallel",), vmem_limit_bytes=VMEM_LIMIT),
        name="qkv_proj",
    )(x2, norm_w, w_in_bf, w_in_bf, w_in_bf, w_in_bf, w_in_bf, w_in_bf, cos_t, sa_t, sb_t)


def _dilated_kernel(bias0_ref, bias_ref, q_ref, kp_ref, kc_ref, vp_ref, vc_ref, o_ref, st_ref):
    n_blocks = q_ref.shape[0] // BAND
    lane = lax.broadcasted_iota(jnp.int32, (BAND, LANES), 1)
    low = lane < HEAD_DIM
    ones = jnp.ones((2 * BAND, LANES), _BF16)

    def band_scores(u, hp):
        rows = slice(u * BAND, (u + 1) * BAND)
        sl = slice(hp * LANES, (hp + 1) * LANES)
        q = q_ref[rows, sl]
        zero = jnp.zeros_like(q)
        qq = jnp.concatenate([jnp.where(low, q, zero), jnp.where(low, zero, q)], axis=0)
        k_prev = kp_ref[:, sl] if u == 0 else kc_ref[(u - 1) * BAND:u * BAND, sl]
        kk = jnp.concatenate([k_prev, kc_ref[rows, sl]], axis=0)
        bias = (bias0_ref if u == 0 else bias_ref)[...]
        return lax.dot_general(qq, kk, (((1,), (1,)), ((), ())),
                               preferred_element_type=_F32) + bias

    work = [(u, hp) for u in range(n_blocks) for hp in range(A_HEADS // 2)]
    s = band_scores(*work[0])
    st = None
    for i, (u, hp) in enumerate(work):
        s_next = band_scores(*work[i + 1]) if i + 1 < len(work) else None
        rows = slice(u * BAND, (u + 1) * BAND)
        sl = slice(hp * LANES, (hp + 1) * LANES)
        if hp == 0:
            st = jnp.zeros((BAND, LANES), _F32)
        v_prev = vp_ref[:, sl] if u == 0 else vc_ref[(u - 1) * BAND:u * BAND, sl]
        va = jnp.concatenate([jnp.concatenate([v_prev, vc_ref[rows, sl]], axis=0), ones],
                             axis=1)
        m = jnp.max(s, axis=1, keepdims=True)
        p = jnp.exp2(s - m).astype(_BF16)
        pv = jnp.dot(p, va, preferred_element_type=_F32)
        num, den = pv[:, :LANES], pv[:, LANES:]
        o_ref[rows, sl] = jnp.where(low, num[:BAND], num[BAND:])
        for e in range(2):
            h = 2 * hp + e
            st = jnp.where(lane == h, m[e * BAND:(e + 1) * BAND], st)
            st = jnp.where(lane == A_HEADS + h, den[e * BAND:(e + 1) * BAND], st)
        if hp == A_HEADS // 2 - 1:
            st_ref[rows, :] = st
        s = s_next


def _dilated_call(qkv, bias, dil, batch, seq):
    per_class = seq // dil
    qb = min(DIL_QB, per_class // BAND)
    rows = qb * BAND
    steps = per_class // rows
    cur = lambda which: pl.BlockSpec((None, None, None, rows, D_MODEL),
                                     lambda b, r, n: (which, b, r, n, 0))
    prev = lambda which: pl.BlockSpec(
        (None, None, None, BAND, D_MODEL),
        lambda b, r, n: (which, b, r, jnp.maximum(n * qb - 1, 0), 0))
    return pl.pallas_call(
        _dilated_kernel,
        out_shape=(jax.ShapeDtypeStruct((batch, dil, per_class, D_MODEL), _F32),
                   jax.ShapeDtypeStruct((batch, dil, per_class, LANES), _F32)),
        grid=(batch, dil, steps),
        in_specs=[
            pl.BlockSpec((None, 2 * BAND, 2 * BAND), lambda b, r, n: (jnp.minimum(n, 1), 0, 0)),
            pl.BlockSpec((None, 2 * BAND, 2 * BAND), lambda b, r, n: (1, 0, 0)),
            cur(0), prev(1), cur(1), prev(2), cur(2),
        ],
        out_specs=(pl.BlockSpec((None, None, rows, D_MODEL), lambda b, r, n: (b, r, n, 0)),
                   pl.BlockSpec((None, None, rows, LANES), lambda b, r, n: (b, r, n, 0))),
        compiler_params=pltpu.CompilerParams(
            dimension_semantics=("parallel", "parallel", "arbitrary"),
            vmem_limit_bytes=VMEM_LIMIT),
        name=f"dilated_d{dil}",
    )(bias, bias, qkv, qkv, qkv, qkv, qkv)


def _band_bias():
    qi = jnp.arange(BAND)[:, None]
    kc = jnp.arange(2 * BAND)[None, :]
    ok = (kc >= qi) & (kc <= qi + BAND)
    first = ok & (kc >= BAND)
    both = jnp.stack([first, ok]).astype(_F32)
    both = jnp.concatenate([both, both], axis=1)
    return (1.0 - both) * NEG


def _diff_kernel(lam_ref, sw_ref, q_ref, k_ref, vt_ref, o_ref, m_ref, acc_ref, *s_refs,
                 tq, tk, lam_init):
    qi = pl.program_id(2)
    lane = lax.broadcasted_iota(jnp.int32, (tq, LANES), 1)
    low = lane < HEAD_DIM
    q = q_ref[...]
    zero = jnp.zeros_like(q)
    qq = jnp.concatenate([jnp.where(low, q, zero), jnp.where(low, zero, q)], axis=0)
    ones = jnp.ones((ONES_ROWS, tk), _BF16)
    sub_blocks = tq // tk

    def chunks(lo):
        n = tq - lo
        return [(c, (lo + c if c < n else tq + lo + c - n), c % n)
                for c in range(0, 2 * n, DIFF_CHUNK)]

    def score_chunk(j, lo, idx):
        c, sc, _ = chunks(lo)[idx]
        start = j * tk if isinstance(j, int) else pl.multiple_of(j * tk, tk)
        s_refs[c // DIFF_CHUNK][...] = lax.dot_general(
            k_ref[pl.ds(start, tk), :], qq[sc:sc + DIFF_CHUNK], (((1,), (1,)), ((), ())),
            preferred_element_type=_F32)

    def update_chunk(j, diag, idx):
        lo = 0 if diag is None else diag * tk
        c, sc, q0 = chunks(lo)[idx]
        cols = slice(sc, sc + DIFF_CHUNK)
        s = s_refs[c // DIFF_CHUNK][...]
        if diag is not None and q0 < tk - 1:
            key = lax.broadcasted_iota(jnp.int32, s.shape, 0)
            qry = lax.broadcasted_iota(jnp.int32, s.shape, 1) + q0
            s = jnp.where(key <= qry, s, NEG)
        m = m_ref[:, cols]
        m_new = jnp.maximum(m, jnp.max(s, axis=0, keepdims=True))
        alpha = jnp.exp2(m - m_new)
        p = jnp.exp2(s - m_new).astype(_BF16)
        va = jnp.concatenate([vt_ref[j], ones], axis=0)
        acc_ref[:, cols] = alpha * acc_ref[:, cols] + jnp.dot(va, p, preferred_element_type=_F32)
        m_ref[:, cols] = m_new

    n_full = len(chunks(0))
    m_ref[...] = jnp.full(m_ref.shape, NEG, _F32)
    acc_ref[...] = jnp.zeros(acc_ref.shape, _F32)
    for idx in range(DIFF_LAG):
        score_chunk(0, 0, idx)

    def body(t, carry):
        for u in range(sub_blocks):
            j = sub_blocks * t + u
            for idx in range(n_full):
                ahead = idx + DIFF_LAG
                if ahead < n_full:
                    score_chunk(j, 0, ahead)
                else:
                    score_chunk(j + 1, 0, ahead - n_full)
                update_chunk(j, None, idx)
        return carry

    j0 = sub_blocks * qi
    lax.fori_loop(0, qi, body, 0)
    flat = [(d, idx) for d in range(sub_blocks) for idx in range(len(chunks(d * tk)))]
    for pos, (d, idx) in enumerate(flat):
        if pos + DIFF_LAG < len(flat):
            d_a, idx_a = flat[pos + DIFF_LAG]
            score_chunk(j0 + d_a, d_a * tk, idx_a)
        update_chunk(j0 + d, d, idx)


    acc = acc_ref[...]
    lp = lam_ref[...]
    lam = (jnp.exp(jnp.sum(lp[0:1] * lp[1:2], axis=1, keepdims=True))
           - jnp.exp(jnp.sum(lp[2:3] * lp[3:4], axis=1, keepdims=True)) + lam_init)
    num, den = acc[:LANES], acc[LANES:LANES + 1]
    ot = num[:, :tq] / den[:, :tq] - lam * (num[:, tq:] / den[:, tq:])
    o = ot.T
    o_ref[...] = _rms(o, sw_ref[...], SUBLN_EPS) * (1.0 - lam_init)


def _diff_call(qk, vt, lam_p, subln_w, layer, batch, seq, tq=DIFF_TQ, tk=DIFF_TK):
    assert tq % tk == 0 and (2 * tk) % DIFF_CHUNK == 0
    assert DIFF_LAG < 2 * min(tq, 2 * tk) // DIFF_CHUNK
    m = batch * seq
    nq = seq // tq
    nk = seq // tk
    kern = functools.partial(_diff_kernel, tq=tq, tk=tk, lam_init=_lambda_init(layer))
    return pl.pallas_call(
        kern,
        out_shape=jax.ShapeDtypeStruct((m, D_MODEL), _F32),
        grid=(batch, B_HEADS, nq),
        in_specs=[
            pl.BlockSpec((None, 4, HEAD_DIM), lambda b, h, i: (layer, 0, 0)),
            pl.BlockSpec((None, 1, LANES), lambda b, h, i: (layer, 0, 0)),
            pl.BlockSpec((None, tq, LANES), lambda b, h, i: (0, b * nq + i, h)),
            pl.BlockSpec((None, seq, LANES), lambda b, h, i: (1, b, h)),
            pl.BlockSpec((nk, LANES, tk), lambda b, h, i: (b, h, 0)),
        ],
        out_specs=pl.BlockSpec((tq, LANES), lambda b, h, i: (b * nq + i, h)),
        scratch_shapes=[pltpu.VMEM((1, 2 * tq), _F32),
                        pltpu.VMEM((LANES + ONES_ROWS, 2 * tq), _F32)]
                       + [pltpu.VMEM((tk, DIFF_CHUNK), _F32)] * (2 * tq // DIFF_CHUNK),
        compiler_params=pltpu.CompilerParams(
            dimension_semantics=("parallel", "parallel", "arbitrary"),
            vmem_limit_bytes=VMEM_LIMIT),
        name="diff_attn",
    )(lam_p, subln_w, qk, qk, vt)


def _expand_heads(w, e):
    hi = w.astype(_BF16)
    lo = (w - hi.astype(_F32)).astype(_BF16)
    return jnp.dot(jnp.concatenate([hi, lo], axis=1), e, preferred_element_type=_F32)


def _merge_kernel(x_ref, o1_ref, o2_ref, o3_ref, s1_ref, s2_ref, s3_ref, ob_ref, nw_ref,
                  wza_ref, wzb_ref, wga_ref, wgb_ref, wpa_ref, wpb_ref, wo_ref, e_ref, fw_ref,
                  out_ref, on2_ref, on3_ref, tmp_ref, sn2_ref, sn3_ref, *, final):
    x = x_ref[...]
    tm = x.shape[0]
    hdn = _rms(x, nw_ref[...], RMS_EPS).astype(_BF16)

    n_blk = D_MODEL // LANES
    d4, d16 = DILATIONS[1], DILATIONS[2]
    step = d16 // d4
    for blk in range(n_blk):
        sl = slice(blk * LANES, (blk + 1) * LANES)
        for r in range(d4):
            on2_ref[blk, pl.ds(r, tm // d4, stride=d4), :] = o2_ref[r, :, sl]
        for r in range(d16):
            c, b = r % d4, r // d4
            tmp_ref[blk, pl.ds(c * (tm // d4) + b, tm // d16, stride=step), :] = o3_ref[r, :, sl]
        for c in range(d4):
            on3_ref[blk, pl.ds(c, tm // d4, stride=d4), :] = (
                tmp_ref[blk, c * (tm // d4):(c + 1) * (tm // d4), :])
    for d, src, dst in ((DILATIONS[1], s2_ref, sn2_ref), (DILATIONS[2], s3_ref, sn3_ref)):
        for r in range(d):
            dst[pl.ds(r, tm // d, stride=d), :] = src[r]
    o1 = o1_ref[...]
    outs = ([o1[:, blk * LANES:(blk + 1) * LANES] for blk in range(n_blk)],
            [on2_ref[blk] for blk in range(n_blk)], [on3_ref[blk] for blk in range(n_blk)])

    sts = (s1_ref[...], sn2_ref[...], sn3_ref[...])
    lane = lax.broadcasted_iota(jnp.int32, sts[0].shape, 1)
    mx = jnp.maximum(jnp.maximum(sts[0], sts[1]), sts[2])
    wts = [jnp.exp2(s - mx) for s in sts]
    dens = [pltpu.roll(s, LANES - A_HEADS, axis=1) * w for s, w in zip(sts, wts)]
    inv = 1.0 / (dens[0] + dens[1] + dens[2])
    e = e_ref[...]
    ya_blocks = None
    for w, o_blocks in zip(wts, outs):
        wn = _expand_heads(jnp.where(lane < A_HEADS, w * inv, 0.0), e)
        terms = [wn[:, blk * LANES:(blk + 1) * LANES] * o_blocks[blk] for blk in range(n_blk)]
        ya_blocks = terms if ya_blocks is None else [a + t for a, t in zip(ya_blocks, terms)]
    ya = jnp.concatenate(ya_blocks, axis=1)

    za = jnp.dot(hdn, wza_ref[...], preferred_element_type=_F32)
    ya = (ya * (za * _sigmoid(za))).astype(_BF16)
    pa = jnp.dot(ya, wpa_ref[...], preferred_element_type=_F32)
    ga = jnp.dot(hdn, wga_ref[...], preferred_element_type=_F32)
    merged = _sigmoid(ga) * pa

    zb = jnp.dot(hdn, wzb_ref[...], preferred_element_type=_F32)
    yb = (ob_ref[...] * (zb * _sigmoid(zb))).astype(_BF16)
    pb = jnp.dot(yb, wpb_ref[...], preferred_element_type=_F32)
    gb = jnp.dot(hdn, wgb_ref[...], preferred_element_type=_F32)
    merged = (merged + _sigmoid(gb) * pb).astype(_BF16)

    y = x + jnp.dot(merged, wo_ref[...], preferred_element_type=_F32)
    if final:
        y = _rms(y, fw_ref[...], RMS_EPS)
    out_ref[...] = y


def _merge_call(x2, os_, sts, ob, norm_w, w_in_bf, wpa_bf, wpb_bf, wo_bf, expand, final_w,
                layer, final, seq, tm=256):
    m = x2.shape[0]
    tiles = seq // tm
    row = lambda width: pl.BlockSpec((tm, width), lambda i: (i, 0))
    pat = lambda d, width: pl.BlockSpec((None, d, tm // d, width),
                                        lambda i: (i // tiles, 0, i % tiles, 0))
    d1, d4, d16 = DILATIONS
    o1 = os_[0].reshape(m, D_MODEL)
    s1 = sts[0].reshape(m, LANES)
    once = pl.Buffered(1)
    w_in_col = lambda c: pl.BlockSpec((None, D_MODEL, D_MODEL), lambda i: (layer, 0, c),
                                      pipeline_mode=once)
    w_sq = pl.BlockSpec((None, D_MODEL, D_MODEL), lambda i: (layer, 0, 0), pipeline_mode=once)
    return pl.pallas_call(
        functools.partial(_merge_kernel, final=final),
        out_shape=jax.ShapeDtypeStruct((m, D_MODEL), _F32),
        grid=(m // tm,),
        in_specs=[
            row(D_MODEL), row(D_MODEL), pat(d4, D_MODEL), pat(d16, D_MODEL),
            row(LANES), pat(d4, LANES), pat(d16, LANES), row(D_MODEL),
            pl.BlockSpec((None, 1, D_MODEL), lambda i: (layer, 0, 0)),
            w_in_col(3), w_in_col(7), w_in_col(8), w_in_col(9),
            w_sq, w_sq, w_sq,
            pl.BlockSpec((2 * LANES, D_MODEL), lambda i: (0, 0), pipeline_mode=once),
            pl.BlockSpec((1, D_MODEL), lambda i: (0, 0)),
        ],
        out_specs=row(D_MODEL),
        scratch_shapes=[pltpu.VMEM((D_MODEL // LANES, tm, LANES), _F32)] * 3
                       + [pltpu.VMEM((tm, LANES), _F32)] * 2,
        compiler_params=pltpu.CompilerParams(
            dimension_semantics=("parallel",), vmem_limit_bytes=VMEM_LIMIT),
        name="merge_out",
    )(x2, o1, os_[1], os_[2], s1, sts[1], sts[2], ob, norm_w, w_in_bf, w_in_bf, w_in_bf, w_in_bf,
      wpa_bf, wpb_bf, wo_bf, expand, final_w)


def _rope_tables(seq):
    half = ROPE_DIM // 2
    inv = 1.0 / (ROPE_THETA ** (jnp.arange(0, ROPE_DIM, 2, dtype=_F32) / ROPE_DIM))
    ang = jnp.arange(seq, dtype=_F32)[:, None] * inv[None, :]
    cos, sin = jnp.cos(ang), jnp.sin(ang)
    rest = HEAD_DIM - ROPE_DIM
    z = lambda n: jnp.zeros((seq, n), _F32)
    cos_h = jnp.concatenate([cos, cos, jnp.ones((seq, rest), _F32)], axis=1)
    sa_h = jnp.concatenate([z(half), sin, z(rest)], axis=1)
    sb_h = jnp.concatenate([-sin, z(half), z(rest)], axis=1)
    rep = LANES // HEAD_DIM
    return tuple(jnp.tile(t, (1, rep)) for t in (cos_h, sa_h, sb_h))


def kernel(x, norm_w, w_in, lambda_q1, lambda_k1, lambda_q2, lambda_k2, subln_w,
           w_proj_a, w_proj_b, w_out, final_norm_w):
    batch, seq, d = x.shape
    depth = norm_w.shape[0]
    assert d == D_MODEL and seq % (BAND * DILATIONS[-1]) == 0
    assert w_in.shape[-1] == N_COL_BLOCKS * D_MODEL

    w_in_bf = w_in.astype(_BF16)
    wpa_bf = w_proj_a.astype(_BF16)
    wpb_bf = w_proj_b.astype(_BF16)
    wo_bf = w_out.astype(_BF16)
    lam_p = jnp.stack([lambda_q1, lambda_k1, lambda_q2, lambda_k2], axis=1)
    tabs = _rope_tables(seq)
    bias = _band_bias()
    head_of_col = jnp.arange(D_MODEL)[None, :] // HEAD_DIM
    expand = (jnp.arange(2 * LANES)[:, None] % LANES == head_of_col).astype(_BF16)
    final_w = final_norm_w.reshape(1, D_MODEL)
    norm_w = norm_w.reshape(depth, 1, D_MODEL)
    subln_w = subln_w.reshape(depth, 1, LANES)

    h = x.reshape(batch * seq, D_MODEL)
    for layer in range(depth):
        *a_layouts, qk, vt = _proj_call(h, norm_w, w_in_bf, layer, tabs, batch, seq)
        os_, sts = zip(*[_dilated_call(a, bias, dil, batch, seq)
                         for a, dil in zip(a_layouts, DILATIONS)])
        ob = _diff_call(qk, vt, lam_p, subln_w, layer, batch, seq)
        h = _merge_call(h, os_, sts, ob, norm_w, w_in_bf, wpa_bf, wpb_bf, wo_bf, expand,
                        final_w, layer, layer == depth - 1, seq)
    return h.reshape(batch, seq, D_MODEL)
```
